```python
import jax
import jax.numpy as jnp
from jax import lax
import numpy as np

D_MODEL = 1024
BATCH = 1
SEQ = 16384
DEPTH = 2

EPS = 1e-6
MLA_HEADS = 8
MLA_Q_LORA = 256
MLA_KV_LORA = 128
MLA_NOPE = 128
MLA_ROPE = 64
MLA_V = 128
MLA_QK = MLA_NOPE + MLA_ROPE
MLA_IN = MLA_Q_LORA + MLA_KV_LORA + MLA_ROPE
ROPE_THETA = 10000.0
Q_BLOCK = 128
MLSTM_HEADS = 8
MLSTM_V = D_MODEL // MLSTM_HEADS
MLSTM_QK = MLSTM_V // 2
MLSTM_QK_TOT = MLSTM_HEADS * MLSTM_QK
MLSTM_V_TOT = MLSTM_HEADS * MLSTM_V
MLSTM_IN = 2 * MLSTM_QK_TOT + MLSTM_V_TOT + D_MODEL + 2 * MLSTM_HEADS
MLSTM_CHUNK = 64
CONV_W = 4
GATE_CAP = 15.0
D_FF = 7 * D_MODEL // 2
N_EXPERTS = 8
TOP_K = 2
N_EVEN = (DEPTH + 1) // 2
N_ODD = DEPTH // 2

kernel_name = 'hybrid_mla_mlstm_moe_trunk'


def rmsnorm(x, g):
    xf = x.astype(jnp.float32)
    y = xf * lax.rsqrt(jnp.mean(xf * xf, axis=-1, keepdims=True) + EPS)
    return (y * g.astype(jnp.float32)).astype(x.dtype)


def rope_tables(positions):
    inv = 1.0 / (ROPE_THETA ** (jnp.arange(0, MLA_ROPE, 2, dtype=jnp.float32) / MLA_ROPE))
    ang = positions.astype(jnp.float32)[..., None] * inv
    return jnp.cos(ang), jnp.sin(ang)


def apply_rope(t, cos, sin):
    tf = t.astype(jnp.float32)
    t1, t2 = jnp.split(tf, 2, axis=-1)
    return jnp.concatenate([t1 * cos - t2 * sin, t1 * sin + t2 * cos], axis=-1).astype(t.dtype)


def causal_block_attention(q, k, v):
    B, S, H, DQ = q.shape
    DV = v.shape[-1]
    nb = S // Q_BLOCK
    scale = DQ ** -0.5
    kf = k.astype(jnp.float32)
    vf = v.astype(jnp.float32)
    qb = q.astype(jnp.float32).reshape(B, nb, Q_BLOCK, H, DQ).transpose(1, 0, 2, 3, 4)
    starts = jnp.arange(nb, dtype=jnp.int32) * Q_BLOCK
    kpos = jnp.arange(S, dtype=jnp.int32)

    def one_block(args):
        q_blk, start = args
        s = jnp.einsum('bqhd,bkhd->bhqk', q_blk, kf) * scale
        qpos = start + jnp.arange(Q_BLOCK, dtype=jnp.int32)
        s = jnp.where(kpos[None, :] <= qpos[:, None], s, -jnp.inf)
        p = jax.nn.softmax(s, axis=-1)
        return jnp.einsum('bhqk,bkhd->bqhd', p, vf)

    out = lax.map(one_block, (qb, starts))
    return out.transpose(1, 0, 2, 3, 4).reshape(B, S, H, DV)


def mla_mixer(xn, cos, sin, w_in, q_norm, w_qb, kv_norm, w_kvb, w_out):
    B, S, _ = xn.shape
    proj = xn @ w_in
    c_q = proj[..., :MLA_Q_LORA]
    c_kv = proj[..., MLA_Q_LORA:MLA_Q_LORA + MLA_KV_LORA]
    k_rope = proj[..., MLA_Q_LORA + MLA_KV_LORA:]
    q = (rmsnorm(c_q, q_norm) @ w_qb).reshape(B, S, MLA_HEADS, MLA_QK)
    kv = (rmsnorm(c_kv, kv_norm) @ w_kvb).reshape(B, S, MLA_HEADS, MLA_NOPE + MLA_V)
    q_rot = apply_rope(q[..., MLA_NOPE:], cos[:, :, None, :], sin[:, :, None, :])
    k_rot = apply_rope(k_rope, cos, sin)[:, :, None, :]
    qh = jnp.concatenate([q[..., :MLA_NOPE], q_rot], axis=-1)
    kh = jnp.concatenate([kv[..., :MLA_NOPE], jnp.broadcast_to(k_rot, (B, S, MLA_HEADS, MLA_ROPE))], axis=-1)
    vh = kv[..., MLA_NOPE:]
    o = causal_block_attention(qh, kh, vh)
    return o.reshape(B, S, MLA_HEADS * MLA_V).astype(xn.dtype) @ w_out


def mlstm_chunkwise(q, k, v, i_pre, log_f):
    B, S, H, DK = q.shape
    DV = v.shape[-1]
    L = MLSTM_CHUNK
    NC = S // L
    f32 = jnp.float32
    q = q.astype(f32).transpose(0, 2, 1, 3).reshape(B, H, NC, L, DK) * (DK ** -0.5)
    k = k.astype(f32).transpose(0, 2, 1, 3).reshape(B, H, NC, L, DK)
    v = v.astype(f32).transpose(0, 2, 1, 3).reshape(B, H, NC, L, DV)
    ig = i_pre.astype(f32).transpose(0, 2, 1).reshape(B, H, NC, L)
    lf = log_f.astype(f32).transpose(0, 2, 1).reshape(B, H, NC, L)
    b = jnp.cumsum(lf, axis=-1)
    b_last = b[..., -1]
    a = b_last[..., None] - b + ig
    m_loc = jnp.max(a, axis=-1)
    w = jnp.exp(a - m_loc[..., None])
    c_loc = jnp.einsum('bhcl,bhcld,bhcle->bhcde', w, k, v)
    n_loc = jnp.einsum('bhcl,bhcld->bhcd', w, k)

    def step(carry, inp):
        c, n, m = carry
        cl, nl, ml, bl = inp
        m_new = jnp.maximum(bl + m, ml)
        s_prev = jnp.exp(bl + m - m_new)
        s_loc = jnp.exp(ml - m_new)
        c_new = s_prev[..., None, None] * c + s_loc[..., None, None] * cl
        n_new = s_prev[..., None] * n + s_loc[..., None] * nl
        return (c_new, n_new, m_new), (c, n, m)

    init = (jnp.zeros((B, H, DK, DV), f32), jnp.zeros((B, H, DK), f32), jnp.zeros((B, H), f32))
    xs = (c_loc.transpose(2, 0, 1, 3, 4), n_loc.transpose(2, 0, 1, 3), m_loc.transpose(2, 0, 1), b_last.transpose(2, 0, 1))
    _, (c_prev, n_prev, m_prev) = lax.scan(step, init, xs)
    c_prev = c_prev.transpose(1, 2, 0, 3, 4)
    n_prev = n_prev.transpose(1, 2, 0, 3)
    m_prev = m_prev.transpose(1, 2, 0)
    causal = jnp.tril(jnp.ones((L, L), dtype=bool))
    d = jnp.where(causal, b[..., :, None] - b[..., None, :] + ig[..., None, :], -jnp.inf)
    inter = b + m_prev[..., None]
    m_t = jnp.maximum(inter, jnp.max(d, axis=-1))
    wts = jnp.exp(d - m_t[..., None]) * jnp.einsum('bhctd,bhcsd->bhcts', q, k)
    s_inter = jnp.exp(inter - m_t)
    num = jnp.einsum('bhcts,bhcse->bhcte', wts, v) + s_inter[..., None] * jnp.einsum('bhctd,bhcde->bhcte', q, c_prev)
    den = jnp.sum(wts, axis=-1) + s_inter * jnp.einsum('bhctd,bhcd->bhct', q, n_prev)
    h = num / jnp.maximum(jnp.abs(den), jnp.exp(-m_t))[..., None]
    return h.reshape(B, H, S, DV).transpose(0, 2, 1, 3)


def mlstm_mixer(xn, w_in, conv_w, conv_b, gate_b, head_norm, w_out):
    B, S, _ = xn.shape
    proj = xn @ w_in
    o1 = 2 * MLSTM_QK_TOT
    o2 = o1 + MLSTM_V_TOT
    o3 = o2 + D_MODEL
    qk_raw = proj[..., :o1]
    v = proj[..., o1:o2]
    o_pre = proj[..., o2:o3]
    g_pre = proj[..., o3:]
    qk = lax.conv_general_dilated(qk_raw, conv_w[:, None, :], window_strides=(1,), padding=[(CONV_W - 1, 0)],
                                  dimension_numbers=('NWC', 'WIO', 'NWC'), feature_group_count=o1)
    qk = jax.nn.silu(qk + conv_b)
    q = qk[..., :MLSTM_QK_TOT].reshape(B, S, MLSTM_HEADS, MLSTM_QK)
    k = qk[..., MLSTM_QK_TOT:].reshape(B, S, MLSTM_HEADS, MLSTM_QK)
    gates = g_pre.astype(jnp.float32) + gate_b.astype(jnp.float32)
    gates = GATE_CAP * jnp.tanh(gates / GATE_CAP)
    i_pre = gates[..., :MLSTM_HEADS]
    log_f = jax.nn.log_sigmoid(gates[..., MLSTM_HEADS:])
    h = mlstm_chunkwise(q, k, v.reshape(B, S, MLSTM_HEADS, MLSTM_V), i_pre, log_f)
    h = rmsnorm(h, head_norm.reshape(MLSTM_HEADS, MLSTM_V)).reshape(B, S, MLSTM_V_TOT)
    y = jax.nn.sigmoid(o_pre.astype(jnp.float32)) * h
    return y.astype(xn.dtype) @ w_out


def swiglu(h, wg, wu, wd):
    return (jax.nn.silu(h @ wg) * (h @ wu)) @ wd


def moe_swiglu(h, router, wg, wu, wd):
    logits = (h @ router).astype(jnp.float32)
    top_v, top_i = lax.top_k(logits, TOP_K)
    gates = jax.nn.softmax(top_v, axis=-1)
    combine = jnp.sum(jax.nn.one_hot(top_i, N_EXPERTS, dtype=jnp.float32) * gates[..., None], axis=-2)
    y = jnp.zeros(h.shape, jnp.float32)
    for e in range(N_EXPERTS):
        y = y + combine[..., e:e + 1] * swiglu(h, wg[e], wu[e], wd[e]).astype(jnp.float32)
    return y.astype(h.dtype)


def setup_inputs(seed: int = 0) -> dict:
    key = jax.random.key(seed)
    ks = jax.random.split(key, 32)
    f32 = jnp.float32

    def nrm(k, shape, scale):
        return jax.random.normal(k, shape, f32) * scale

    x = nrm(ks[0], (BATCH, SEQ, D_MODEL), 1.0)
    offs = jax.random.randint(ks[1], (BATCH, 1), 0, 1024, dtype=jnp.int32)
    positions = offs + jnp.arange(SEQ, dtype=jnp.int32)[None, :]
    norm_mix = 1.0 + nrm(ks[2], (DEPTH, D_MODEL), 0.02)
    norm_ffn = 1.0 + nrm(ks[3], (DEPTH, D_MODEL), 0.02)
    final_norm = 1.0 + nrm(ks[4], (D_MODEL,), 0.02)
    mla_w_in = nrm(ks[5], (N_EVEN, D_MODEL, MLA_IN), D_MODEL ** -0.5)
    mla_q_norm = 1.0 + nrm(ks[6], (N_EVEN, MLA_Q_LORA), 0.02)
    mla_w_qb = nrm(ks[7], (N_EVEN, MLA_Q_LORA, MLA_HEADS * MLA_QK), MLA_Q_LORA ** -0.5)
    mla_kv_norm = 1.0 + nrm(ks[8], (N_EVEN, MLA_KV_LORA), 0.02)
    mla_w_kvb = nrm(ks[9], (N_EVEN, MLA_KV_LORA, MLA_HEADS * (MLA_NOPE + MLA_V)), MLA_KV_LORA ** -0.5)
    mla_w_out = nrm(ks[10], (N_EVEN, MLA_HEADS * MLA_V, D_MODEL), (MLA_HEADS * MLA_V) ** -0.5)
    mlstm_w_in = nrm(ks[11], (N_ODD, D_MODEL, MLSTM_IN), D_MODEL ** -0.5)
    mlstm_conv_w = nrm(ks[12], (N_ODD, CONV_W, 2 * MLSTM_QK_TOT), CONV_W ** -0.5)
    mlstm_conv_b = nrm(ks[13], (N_ODD, 2 * MLSTM_QK_TOT), 0.02)
    i_bias = nrm(ks[14], (N_ODD, MLSTM_HEADS), 0.1)
    f_bias = jnp.linspace(3.0, 6.0, MLSTM_HEADS, dtype=f32)[None, :] + nrm(ks[15], (N_ODD, MLSTM_HEADS), 0.1)
    mlstm_gate_b = jnp.concatenate([i_bias, f_bias], axis=-1)
    mlstm_head_norm = 1.0 + nrm(ks[16], (N_ODD, MLSTM_V_TOT), 0.02)
    mlstm_w_out = nrm(ks[17], (N_ODD, MLSTM_V_TOT, D_MODEL), MLSTM_V_TOT ** -0.5)
    ffn_w_gate = nrm(ks[18], (N_EVEN, D_MODEL, D_FF), D_MODEL ** -0.5)
    ffn_w_up = nrm(ks[19], (N_EVEN, D_MODEL, D_FF), D_MODEL ** -0.5)
    ffn_w_down = nrm(ks[20], (N_EVEN, D_FF, D_MODEL), D_FF ** -0.5)
    moe_router = nrm(ks[21], (N_ODD, D_MODEL, N_EXPERTS), D_MODEL ** -0.5)
    moe_w_gate = nrm(ks[22], (N_ODD, N_EXPERTS, D_MODEL, D_FF), D_MODEL ** -0.5)
    moe_w_up = nrm(ks[23], (N_ODD, N_EXPERTS, D_MODEL, D_FF), D_MODEL ** -0.5)
    moe_w_down = nrm(ks[24], (N_ODD, N_EXPERTS, D_FF, D_MODEL), D_FF ** -0.5)
    return {'x': x, 'positions': positions, 'norm_mix': norm_mix, 'norm_ffn': norm_ffn, 'final_norm': final_norm,
            'mla_w_in': mla_w_in, 'mla_q_norm': mla_q_norm, 'mla_w_qb': mla_w_qb, 'mla_kv_norm': mla_kv_norm,
            'mla_w_kvb': mla_w_kvb, 'mla_w_out': mla_w_out, 'mlstm_w_in': mlstm_w_in, 'mlstm_conv_w': mlstm_conv_w,
            'mlstm_conv_b': mlstm_conv_b, 'mlstm_gate_b': mlstm_gate_b, 'mlstm_head_norm': mlstm_head_norm,
            'mlstm_w_out': mlstm_w_out, 'ffn_w_gate': ffn_w_gate, 'ffn_w_up': ffn_w_up, 'ffn_w_down': ffn_w_down,
            'moe_router': moe_router, 'moe_w_gate': moe_w_gate, 'moe_w_up': moe_w_up, 'moe_w_down': moe_w_down}


def reference(x, positions, norm_mix, norm_ffn, final_norm, mla_w_in, mla_q_norm, mla_w_qb, mla_kv_norm,
              mla_w_kvb, mla_w_out, mlstm_w_in, mlstm_conv_w, mlstm_conv_b, mlstm_gate_b, mlstm_head_norm,
              mlstm_w_out, ffn_w_gate, ffn_w_up, ffn_w_down, moe_router, moe_w_gate, moe_w_up, moe_w_down):
    cos, sin = rope_tables(positions)
    for layer in range(DEPTH):
        j = layer // 2
        h = rmsnorm(x, norm_mix[layer])
        if layer % 2 == 0:
            y = mla_mixer(h, cos, sin, mla_w_in[j], mla_q_norm[j], mla_w_qb[j], mla_kv_norm[j], mla_w_kvb[j], mla_w_out[j])
        else:
            y = mlstm_mixer(h, mlstm_w_in[j], mlstm_conv_w[j], mlstm_conv_b[j], mlstm_gate_b[j], mlstm_head_norm[j], mlstm_w_out[j])
        x = x + y.astype(x.dtype)
        h = rmsnorm(x, norm_ffn[layer])
        if layer % 2 == 0:
            y = swiglu(h, ffn_w_gate[j], ffn_w_up[j], ffn_w_down[j])
        else:
            y = moe_swiglu(h, moe_router[j], moe_w_gate[j], moe_w_up[j], moe_w_down[j])
        x = x + y.astype(x.dtype)
    return rmsnorm(x, final_norm)
```

```python
import functools

import jax
import jax.numpy as jnp
from jax import lax
from jax.experimental import pallas as pl
from jax.experimental.pallas import tpu as pltpu

F32 = jnp.float32
BF16 = jnp.bfloat16
I32 = jnp.int32

D_MODEL = 1024
EPS = 1e-6
MLA_HEADS = 8
MLA_Q_LORA = 256
MLA_KV_LORA = 128
MLA_NOPE = 128
MLA_ROPE = 64
MLA_V = 128
MLA_QK = MLA_NOPE + MLA_ROPE
MLA_QK_PAD = 256
ROPE_THETA = 10000.0
MLSTM_HEADS = 8
MLSTM_V = 128
MLSTM_QK = 64
MLSTM_QK_TOT = MLSTM_HEADS * MLSTM_QK
CONV_W = 4
GATE_CAP = 15.0
D_FF = 3584
N_EXPERTS = 8
TOP_K = 2

NEG = -1e30

TOKEN_TILE = 512
ATTN_TILE = 512
MLSTM_CHUNK = 256
MOE_ROW_TILE = 512
FF_CHUNK = 896
HALO = 8
VMEM_LIMIT = 56 * 1024 * 1024


def _dot(a, b):
    return jnp.dot(a, b, preferred_element_type=F32)


def _dot_nt(a, b):
    return lax.dot_general(a, b, (((1,), (1,)), ((), ())), preferred_element_type=F32)


def _dot_tn(a, b):
    return lax.dot_general(a, b, (((0,), (0,)), ((), ())), preferred_element_type=F32)


def _rms(x, g):
    return x * lax.rsqrt(jnp.mean(x * x, axis=-1, keepdims=True) + EPS) * g


def _silu(x):
    return x * jax.nn.sigmoid(x)


def _split3(x):
    hi = x.astype(BF16)
    r1 = x - hi.astype(F32)
    mid = r1.astype(BF16)
    lo = (r1 - mid.astype(F32)).astype(BF16)
    return hi, mid, lo


def _params(n_axes):
    return pltpu.CompilerParams(dimension_semantics=("arbitrary",) * n_axes, vmem_limit_bytes=VMEM_LIMIT)


def _const_spec(shape):
    return pl.BlockSpec(shape, lambda *_: (0,) * len(shape))


def _mla_proj_kernel(x_ref, posc_ref, posr_ref, g_ref, invr_ref, invc_ref, sgnr_ref, sgnc_ref, win_ref, qn_ref,
                     wqt_ref, kvn_ref, wk_ref, wvt_ref, qt_ref, k_ref, vt_ref):
    tm = x_ref.shape[0]
    xn = _rms(x_ref[...], g_ref[...]).astype(BF16)
    proj = _dot(xn, win_ref[...])
    cq = _rms(proj[:, :MLA_Q_LORA], qn_ref[...]).astype(BF16)
    ckv = _rms(proj[:, MLA_Q_LORA:MLA_Q_LORA + MLA_KV_LORA], kvn_ref[...]).astype(BF16)
    ang_r = posc_ref[...].astype(F32) * invr_ref[...]
    cos_r = jnp.cos(ang_r)
    sin_r = jnp.sin(ang_r) * sgnr_ref[...]
    ang_c = invc_ref[...] * posr_ref[...].astype(F32)
    cos_c = jnp.cos(ang_c)
    sin_c = jnp.sin(ang_c) * sgnc_ref[...]
    scale = MLA_QK ** -0.5
    n_nope = MLA_HEADS * MLA_NOPE
    n_rope = MLA_HEADS * MLA_ROPE
    q_t = _dot_nt(wqt_ref[...], cq)
    zq = jnp.zeros((MLA_QK_PAD - MLA_QK, tm), BF16)
    for h in range(MLA_HEADS):
        qt_ref[h, 0:MLA_NOPE, :] = (q_t[h * MLA_NOPE:(h + 1) * MLA_NOPE, :] * scale).astype(BF16)
        a = q_t[n_nope + h * MLA_ROPE:n_nope + (h + 1) * MLA_ROPE, :]
        b = q_t[n_nope + n_rope + h * MLA_ROPE:n_nope + n_rope + (h + 1) * MLA_ROPE, :]
        qt_ref[h, MLA_NOPE:MLA_QK, :] = ((a * cos_c + b * sin_c) * scale).astype(BF16)
        qt_ref[h, MLA_QK:MLA_QK_PAD, :] = zq
    k_nope = _dot(ckv, wk_ref[...])
    o_r = MLA_Q_LORA + MLA_KV_LORA
    k_rot = proj[:, o_r:o_r + MLA_ROPE] * cos_r + proj[:, o_r + MLA_ROPE:o_r + 2 * MLA_ROPE] * sin_r
    k_tail = jnp.concatenate([k_rot, jnp.zeros_like(k_rot)], axis=-1).astype(BF16)
    v_t = _dot_nt(wvt_ref[...], ckv)
    for h in range(MLA_HEADS):
        k_ref[h, :, 0:MLA_NOPE] = k_nope[:, h * MLA_NOPE:(h + 1) * MLA_NOPE].astype(BF16)
        k_ref[h, :, MLA_NOPE:MLA_QK_PAD] = k_tail
        vt_ref[h, 0] = v_t[h * MLA_V:(h + 1) * MLA_V, :].astype(BF16)


def _mla_proj(x, pos, g, w_in, q_norm, w_qb, kv_norm, w_kvb):
    s = x.shape[0]
    tm = ATTN_TILE
    nt = s // tm
    half = MLA_ROPE // 2
    o_r = MLA_Q_LORA + MLA_KV_LORA
    win = jnp.concatenate([w_in, w_in[:, o_r + half:], w_in[:, o_r:o_r + half]], axis=1).astype(BF16)
    wq = w_qb.reshape(MLA_Q_LORA, MLA_HEADS, MLA_QK)
    wq_rope = wq[:, :, MLA_NOPE:]
    wq_swap = jnp.concatenate([wq_rope[..., half:], wq_rope[..., :half]], axis=-1)
    wqt = jnp.concatenate([wq[:, :, :MLA_NOPE].reshape(MLA_Q_LORA, -1), wq_rope.reshape(MLA_Q_LORA, -1),
                           wq_swap.reshape(MLA_Q_LORA, -1)], axis=1).T.astype(BF16)
    wkv = w_kvb.reshape(MLA_KV_LORA, MLA_HEADS, MLA_NOPE + MLA_V)
    wk = wkv[:, :, :MLA_NOPE].reshape(MLA_KV_LORA, -1).astype(BF16)
    wvt = wkv[:, :, MLA_NOPE:].reshape(MLA_KV_LORA, -1).T.astype(BF16)
    inv = 1.0 / (ROPE_THETA ** (jnp.arange(0, MLA_ROPE, 2, dtype=F32) / MLA_ROPE))
    inv2 = jnp.concatenate([inv, inv])
    sgn = jnp.concatenate([-jnp.ones((half,), F32), jnp.ones((half,), F32)])
    n_q = wqt.shape[0]
    return pl.pallas_call(
        _mla_proj_kernel,
        grid=(nt,),
        in_specs=[
            pl.BlockSpec((tm, D_MODEL), lambda i: (i, 0)),
            pl.BlockSpec((tm, 1), lambda i: (i, 0)),
            pl.BlockSpec((1, tm), lambda i: (0, i)),
            _const_spec((1, D_MODEL)),
            _const_spec((1, MLA_ROPE)), _const_spec((MLA_ROPE, 1)),
            _const_spec((1, MLA_ROPE)), _const_spec((MLA_ROPE, 1)),
            _const_spec((D_MODEL, 512)),
            _const_spec((1, MLA_Q_LORA)),
            _const_spec((n_q, MLA_Q_LORA)),
            _const_spec((1, MLA_KV_LORA)),
            _const_spec((MLA_KV_LORA, MLA_HEADS * MLA_NOPE)),
            _const_spec((MLA_HEADS * MLA_V, MLA_KV_LORA)),
        ],
        out_specs=[
            pl.BlockSpec((MLA_HEADS, MLA_QK_PAD, tm), lambda i: (0, 0, i)),
            pl.BlockSpec((MLA_HEADS, tm, MLA_QK_PAD), lambda i: (0, i, 0)),
            pl.BlockSpec((MLA_HEADS, 1, MLA_V, tm), lambda i: (0, i, 0, 0)),
        ],
        out_shape=[
            jax.ShapeDtypeStruct((MLA_HEADS, MLA_QK_PAD, s), BF16),
            jax.ShapeDtypeStruct((MLA_HEADS, s, MLA_QK_PAD), BF16),
            jax.ShapeDtypeStruct((MLA_HEADS, nt, MLA_V, tm), BF16),
        ],
        compiler_params=_params(1),
        name="mla_proj",
    )(x, pos.reshape(s, 1), pos.reshape(1, s), g.reshape(1, -1), inv2.reshape(1, -1), inv2.reshape(-1, 1),
      sgn.reshape(1, -1), sgn.reshape(-1, 1), win, q_norm.reshape(1, -1), wqt, kv_norm.reshape(1, -1), wk, wvt)


def _attn_kernel(qt_ref, k_ref, vt_ref, o_ref, m_sc, l_sc, acc_sc):
    t = qt_ref.shape[2]
    i = pl.program_id(1)
    q_t = qt_ref[0]
    m_sc[...] = jnp.full(m_sc.shape, NEG, F32)
    l_sc[...] = jnp.zeros(l_sc.shape, F32)
    acc_sc[...] = jnp.zeros(acc_sc.shape, F32)

    def step(j, diagonal):
        kj = k_ref[0, pl.ds(pl.multiple_of(j * t, t), t), :]
        s = _dot(kj, q_t)
        if diagonal:
            kpos = lax.broadcasted_iota(I32, (t, t), 0)
            qpos = lax.broadcasted_iota(I32, (t, t), 1)
            s = jnp.where(kpos <= qpos, s, NEG)
        m_old = m_sc[...]
        m_new = jnp.maximum(m_old, jnp.max(s, axis=0, keepdims=True))
        p = jnp.exp(s - m_new)
        alpha = jnp.exp(m_old - m_new)
        l_sc[...] = alpha * l_sc[...] + jnp.sum(p, axis=0, keepdims=True)
        acc_sc[...] = alpha * acc_sc[...] + _dot(vt_ref[0, j], p.astype(BF16))
        m_sc[...] = m_new

    def body(j, carry):
        step(j, False)
        return carry

    lax.fori_loop(0, i, body, 0)
    step(i, True)
    o_t = acc_sc[...] * (1.0 / l_sc[...])
    o_ref[...] = o_t.T.astype(BF16)


def _mla_attn(q_t, k, v_t):
    s = k.shape[1]
    t = ATTN_TILE
    nt = s // t
    return pl.pallas_call(
        _attn_kernel,
        grid=(MLA_HEADS, nt),
        in_specs=[
            pl.BlockSpec((1, MLA_QK_PAD, t), lambda h, i: (h, 0, i)),
            pl.BlockSpec((1, s, MLA_QK_PAD), lambda h, i: (h, 0, 0)),
            pl.BlockSpec((1, nt, MLA_V, t), lambda h, i: (h, 0, 0, 0)),
        ],
        out_specs=pl.BlockSpec((t, MLA_V), lambda h, i: (i, h)),
        out_shape=jax.ShapeDtypeStruct((s, MLA_HEADS * MLA_V), BF16),
        scratch_shapes=[pltpu.VMEM((1, t), F32), pltpu.VMEM((1, t), F32), pltpu.VMEM((MLA_V, t), F32)],
        compiler_params=_params(2),
        name="mla_attn",
    )(q_t, k, v_t)


def _attn_out_ffn_kernel(x_ref, o_ref, wo_ref, g_ref, wg_ref, wu_ref, wd_ref, out_ref):
    x1 = x_ref[...] + _dot(o_ref[...], wo_ref[...])
    h = _rms(x1, g_ref[...]).astype(BF16)
    acc = x1
    for c in range(D_FF // FF_CHUNK):
        sl = slice(c * FF_CHUNK, (c + 1) * FF_CHUNK)
        a = (_silu(_dot(h, wg_ref[:, sl])) * _dot(h, wu_ref[:, sl])).astype(BF16)
        acc = acc + _dot(a, wd_ref[sl, :])
    out_ref[...] = acc


def _resident_spec(shape):
    return pl.BlockSpec(shape, lambda *_: (0,) * len(shape), pipeline_mode=pl.Buffered(1))


def _attn_out_ffn(x, o, w_out, g, wg, wu, wd):
    s = x.shape[0]
    tm = TOKEN_TILE
    return pl.pallas_call(
        _attn_out_ffn_kernel,
        grid=(s // tm,),
        in_specs=[
            pl.BlockSpec((tm, D_MODEL), lambda i: (i, 0)),
            pl.BlockSpec((tm, D_MODEL), lambda i: (i, 0)),
            _resident_spec((D_MODEL, D_MODEL)),
            _const_spec((1, D_MODEL)),
            _resident_spec((D_MODEL, D_FF)),
            _resident_spec((D_MODEL, D_FF)),
            _resident_spec((D_FF, D_MODEL)),
        ],
        out_specs=pl.BlockSpec((tm, D_MODEL), lambda i: (i, 0)),
        out_shape=jax.ShapeDtypeStruct((s, D_MODEL), F32),
        compiler_params=_params(1),
        name="attn_out_ffn",
    )(x, o, w_out.astype(BF16), g.reshape(1, -1), wg.astype(BF16), wu.astype(BF16), wd.astype(BF16))


def _gate_act(g, is_input_gate):
    g = GATE_CAP * jnp.tanh(g / GATE_CAP)
    log_f = jnp.minimum(g, 0.0) - jnp.log1p(jnp.exp(-jnp.abs(g)))
    return jnp.where(is_input_gate, g, log_f)


def _mlstm_proj_kernel(x_ref, g_ref, wqk_ref, wv_ref, wo_ref, wgc_ref, wgt_ref, cw_ref, cb_ref, gbc_ref, gbr_ref,
                       q_ref, k_ref, v_ref, op_ref, gc_ref, gt_ref, ext_sc):
    tm = x_ref.shape[0]

    @pl.when(pl.program_id(0) == 0)
    def _():
        ext_sc[0:HALO, :] = jnp.zeros((HALO, ext_sc.shape[1]), F32)

    xn = _rms(x_ref[...], g_ref[...]).astype(BF16)
    ext_sc[HALO:HALO + tm, :] = _dot(xn, wqk_ref[...])
    conv = cb_ref[...] + cw_ref[CONV_W - 1:CONV_W, :] * ext_sc[HALO:HALO + tm, :]
    for back in range(1, CONV_W):
        w_row = cw_ref[CONV_W - 1 - back:CONV_W - back, :]
        conv = conv + w_row * ext_sc[HALO - back:HALO - back + tm, :]
    ext_sc[0:HALO, :] = ext_sc[tm:tm + HALO, :]
    qk = _silu(conv)
    q_ref[...] = (qk[:, :MLSTM_QK_TOT] * (MLSTM_QK ** -0.5)).astype(BF16)
    k_ref[...] = qk[:, MLSTM_QK_TOT:].astype(BF16)
    v_ref[...] = _dot(xn, wv_ref[...]).astype(BF16)
    op_ref[...] = _dot(xn, wo_ref[...])
    gc = _dot(xn, wgc_ref[...]) + gbc_ref[...]
    gc_ref[...] = _gate_act(gc, lax.broadcasted_iota(I32, gc.shape, 1) < MLSTM_HEADS)
    gt = _dot_nt(wgt_ref[...], xn) + gbr_ref[...]
    gt_ref[...] = _gate_act(gt, lax.broadcasted_iota(I32, gt.shape, 0) < MLSTM_HEADS)


def _mlstm_proj(x, g, w_in, conv_w, conv_b, gate_b):
    s = x.shape[0]
    tm = TOKEN_TILE
    o1 = 2 * MLSTM_QK_TOT
    o2 = o1 + MLSTM_HEADS * MLSTM_V
    o3 = o2 + D_MODEL
    ng = 2 * MLSTM_HEADS
    w = w_in.astype(BF16)
    return pl.pallas_call(
        _mlstm_proj_kernel,
        grid=(s // tm,),
        in_specs=[
            pl.BlockSpec((tm, D_MODEL), lambda i: (i, 0)),
            _const_spec((1, D_MODEL)),
            _const_spec((D_MODEL, o1)), _const_spec((D_MODEL, o2 - o1)), _const_spec((D_MODEL, o3 - o2)),
            _const_spec((D_MODEL, ng)), _const_spec((ng, D_MODEL)),
            _const_spec((CONV_W, o1)), _const_spec((1, o1)),
            _const_spec((1, ng)), _const_spec((ng, 1)),
        ],
        out_specs=[
            pl.BlockSpec((tm, MLSTM_QK_TOT), lambda i: (i, 0)),
            pl.BlockSpec((tm, MLSTM_QK_TOT), lambda i: (i, 0)),
            pl.BlockSpec((tm, o2 - o1), lambda i: (i, 0)),
            pl.BlockSpec((tm, D_MODEL), lambda i: (i, 0)),
            pl.BlockSpec((tm, ng), lambda i: (i, 0)),
            pl.BlockSpec((ng, tm), lambda i: (0, i)),
        ],
        out_shape=[
            jax.ShapeDtypeStruct((s, MLSTM_QK_TOT), BF16),
            jax.ShapeDtypeStruct((s, MLSTM_QK_TOT), BF16),
            jax.ShapeDtypeStruct((s, o2 - o1), BF16),
            jax.ShapeDtypeStruct((s, D_MODEL), F32),
            jax.ShapeDtypeStruct((s, ng), F32),
            jax.ShapeDtypeStruct((ng, s), F32),
        ],
        scratch_shapes=[pltpu.VMEM((tm + HALO, o1), F32)],
        compiler_params=_params(1),
        name="mlstm_proj",
    )(x, g.reshape(1, -1), w[:, :o1], w[:, o1:o2], w[:, o2:o3], w[:, o3:], w[:, o3:].T, conv_w,
      conv_b.reshape(1, -1), gate_b.reshape(1, -1), gate_b.reshape(-1, 1))


def _mlstm_cell_kernel(q_ref, k_ref, v_ref, op_ref, gc_ref, gt_ref, hn_ref, tri_ref, y_ref, c_sc, m_sc):
    n = q_ref.shape[0]
    nh = MLSTM_HEADS

    @pl.when(pl.program_id(0) == 0)
    def _():
        c_sc[...] = jnp.zeros(c_sc.shape, F32)
        m_sc[...] = jnp.zeros(m_sc.shape, F32)

    gc = gc_ref[...]
    gt = gt_ref[...]
    ig_c, lf_c = gc[:, :nh], gc[:, nh:]
    ig_r, lf_r = gt[:nh, :], gt[nh:, :]
    tri = tri_ref[...]
    b_c = sum(_dot(tri, part) for part in _split3(lf_c))
    b_r = sum(_dot_nt(part, tri) for part in _split3(lf_r))
    b_last = b_c[n - 1:n, :]
    m_prev = m_sc[...]
    a_c = b_last - b_c + ig_c
    m_loc = jnp.max(a_c, axis=0, keepdims=True)
    w_c = jnp.exp(a_c - m_loc)
    m_new = jnp.maximum(b_last + m_prev, m_loc)
    s_prev = jnp.exp(b_last + m_prev - m_new)
    s_loc = jnp.exp(m_loc - m_new)
    inter_c = b_c + m_prev
    r_r = ig_r - b_r
    causal = lax.broadcasted_iota(I32, (n, n), 1) <= lax.broadcasted_iota(I32, (n, n), 0)
    ones_col = (lax.broadcasted_iota(I32, (n, MLSTM_V), 1) == 0).astype(BF16)
    for h in range(nh):
        qh = q_ref[:, h * MLSTM_QK:(h + 1) * MLSTM_QK]
        kh = k_ref[:, h * MLSTM_QK:(h + 1) * MLSTM_QK]
        v_ext = jnp.concatenate([v_ref[:, h * MLSTM_V:(h + 1) * MLSTM_V], ones_col], axis=-1)
        d = jnp.where(causal, b_c[:, h:h + 1] + r_r[h:h + 1, :], NEG)
        inter = inter_c[:, h:h + 1]
        m_t = jnp.maximum(inter, jnp.max(d, axis=1, keepdims=True))
        wts = (jnp.exp(d - m_t) * _dot_nt(qh, kh)).astype(BF16)
        c_prev = c_sc[h]
        num = _dot(wts, v_ext) + jnp.exp(inter - m_t) * _dot(qh, c_prev.astype(BF16))
        den = num[:, MLSTM_V:MLSTM_V + 1]
        hh = num[:, :MLSTM_V] / jnp.maximum(jnp.abs(den), jnp.exp(-m_t))
        hh = _rms(hh, hn_ref[:, h * MLSTM_V:(h + 1) * MLSTM_V])
        y = jax.nn.sigmoid(op_ref[:, h * MLSTM_V:(h + 1) * MLSTM_V]) * hh
        y_ref[:, h * MLSTM_V:(h + 1) * MLSTM_V] = y.astype(BF16)
        kw = (kh.astype(F32) * w_c[:, h:h + 1]).astype(BF16)
        c_sc[h] = s_prev[:, h:h + 1] * c_prev + s_loc[:, h:h + 1] * _dot_tn(kw, v_ext)
    m_sc[...] = m_new


def _mlstm_cell(q, k, v, o_pre, gc, gt, head_norm):
    s = q.shape[0]
    n = MLSTM_CHUNK
    ng = 2 * MLSTM_HEADS
    tri = jnp.tril(jnp.ones((n, n), BF16))
    row = lambda w: pl.BlockSpec((n, w), lambda c: (c, 0))
    return pl.pallas_call(
        _mlstm_cell_kernel,
        grid=(s // n,),
        in_specs=[row(MLSTM_QK_TOT), row(MLSTM_QK_TOT), row(D_MODEL), row(D_MODEL), row(ng),
                  pl.BlockSpec((ng, n), lambda c: (0, c)), _const_spec((1, D_MODEL)), _const_spec((n, n))],
        out_specs=row(D_MODEL),
        out_shape=jax.ShapeDtypeStruct((s, D_MODEL), BF16),
        scratch_shapes=[pltpu.VMEM((MLSTM_HEADS, MLSTM_QK, 2 * MLSTM_V), F32), pltpu.VMEM((1, MLSTM_HEADS), F32)],
        compiler_params=_params(1),
        name="mlstm_cell",
    )(q, k, v, o_pre, gc, gt, head_norm.reshape(1, -1), tri)


def _mlstm_out_router_kernel(x_ref, y_ref, wo_ref, g_ref, rt_ref, x3_ref, hn_ref, idx_ref, gate_ref):
    x3 = x_ref[...] + _dot(y_ref[...], wo_ref[...])
    hn = _rms(x3, g_ref[...])
    x3_ref[...] = x3
    hn_ref[...] = hn
    logits = lax.dot_general(rt_ref[...], hn, (((1,), (1,)), ((), ())), precision=lax.Precision.HIGHEST,
                             preferred_element_type=F32)
    e_iota = lax.broadcasted_iota(I32, logits.shape, 0)
    m1 = jnp.max(logits, axis=0, keepdims=True)
    i1 = jnp.min(jnp.where(logits == m1, e_iota, N_EXPERTS), axis=0, keepdims=True)
    rest = jnp.where(e_iota == i1, -jnp.inf, logits)
    m2 = jnp.max(rest, axis=0, keepdims=True)
    i2 = jnp.min(jnp.where(rest == m2, e_iota, N_EXPERTS), axis=0, keepdims=True)
    e2 = jnp.exp(m2 - m1)
    g1 = 1.0 / (1.0 + e2)
    idx_ref[...] = jnp.where(e_iota == 0, i1, jnp.where(e_iota == 1, i2, 0))
    gate_ref[...] = jnp.where(e_iota == 0, g1, jnp.where(e_iota == 1, e2 * g1, 0.0))


def _mlstm_out_router(x, y, w_out, g, router):
    s = x.shape[0]
    tm = TOKEN_TILE
    tok = pl.BlockSpec((tm, D_MODEL), lambda i: (i, 0))
    lane = pl.BlockSpec((N_EXPERTS, tm), lambda i: (0, i))
    return pl.pallas_call(
        _mlstm_out_router_kernel,
        grid=(s // tm,),
        in_specs=[tok, tok, _const_spec((D_MODEL, D_MODEL)), _const_spec((1, D_MODEL)),
                  _const_spec((N_EXPERTS, D_MODEL))],
        out_specs=[tok, tok, lane, lane],
        out_shape=[jax.ShapeDtypeStruct((s, D_MODEL), F32), jax.ShapeDtypeStruct((s, D_MODEL), F32),
                   jax.ShapeDtypeStruct((N_EXPERTS, s), I32), jax.ShapeDtypeStruct((N_EXPERTS, s), F32)],
        compiler_params=_params(1),
        name="mlstm_out_router",
    )(x, y, w_out.astype(BF16), g.reshape(1, -1), router.T)


def _row_gather_copy(src_hbm, dst, sem, src_row, dst_row):
    return pltpu.make_async_copy(src_hbm.at[pl.ds(src_row, 1), :], dst.at[pl.ds(dst_row, 1), :], sem)


def _moe_kernel(te_ref, nu_ref, tok_ref, tok_next_ref, hn_hbm, wg_ref, wu_ref, wd_ref, y_ref, xf_sc, xb_sc, sem):
    del te_ref
    tr = xf_sc.shape[0]
    t = pl.program_id(0)
    c = pl.program_id(1)
    n_used = nu_ref[0]

    def issue(idx_ref):
        def body(r, carry):
            _row_gather_copy(hn_hbm, xf_sc, sem, idx_ref[0, 0, r], r).start()
            return carry
        lax.fori_loop(0, tr, body, 0, unroll=8)

    @pl.when((c == 0) & (t == 0) & (n_used > 0))
    def _():
        issue(tok_ref)

    @pl.when((c == 0) & (t < n_used))
    def _():
        pltpu.make_async_copy(hn_hbm.at[pl.ds(0, tr), :], xf_sc, sem).wait()
        xb_sc[...] = xf_sc[...].astype(BF16)

    @pl.when((c == 0) & (t + 1 < n_used))
    def _():
        issue(tok_next_ref)

    @pl.when(c == 0)
    def _():
        y_ref[...] = jnp.zeros(y_ref.shape, F32)

    @pl.when(t < n_used)
    def _():
        xb = xb_sc[...]
        a = (_silu(_dot(xb, wg_ref[0])) * _dot(xb, wu_ref[0])).astype(BF16)
        y_ref[...] += _dot(a, wd_ref[0])


def _moe_experts(hn, tok_sorted, tile_expert, n_used, wg, wu, wd):
    r = tok_sorted.shape[0]
    tr = MOE_ROW_TILE
    nt = r // tr
    nc = D_FF // FF_CHUNK
    tok3 = tok_sorted.reshape(nt, 1, tr)
    grid_spec = pltpu.PrefetchScalarGridSpec(
        num_scalar_prefetch=2,
        grid=(nt, nc),
        in_specs=[
            pl.BlockSpec((1, 1, tr), lambda t, c, te, nu: (t, 0, 0), memory_space=pltpu.SMEM),
            pl.BlockSpec((1, 1, tr), lambda t, c, te, nu: (jnp.minimum(t + 1, nt - 1), 0, 0),
                         memory_space=pltpu.SMEM),
            pl.BlockSpec(memory_space=pl.ANY),
            pl.BlockSpec((1, D_MODEL, FF_CHUNK), lambda t, c, te, nu: (te[t], 0, c)),
            pl.BlockSpec((1, D_MODEL, FF_CHUNK), lambda t, c, te, nu: (te[t], 0, c)),
            pl.BlockSpec((1, FF_CHUNK, D_MODEL), lambda t, c, te, nu: (te[t], c, 0)),
        ],
        out_specs=pl.BlockSpec((tr, D_MODEL), lambda t, c, te, nu: (t, 0)),
        scratch_shapes=[pltpu.VMEM((tr, D_MODEL), F32), pltpu.VMEM((tr, D_MODEL), BF16), pltpu.SemaphoreType.DMA],
    )
    return pl.pallas_call(
        _moe_kernel,
        grid_spec=grid_spec,
        out_shape=jax.ShapeDtypeStruct((r, D_MODEL), F32),
        compiler_params=_params(2),
        name="moe_experts",
    )(tile_expert, n_used, tok3, tok3, hn, wg.astype(BF16), wu.astype(BF16), wd.astype(BF16))


def _combine_kernel(r1_ref, r2_ref, r1n_ref, r2n_ref, x_ref, gate_ref, g_ref, y_hbm, out_ref, buf, sem):
    tm = x_ref.shape[0]
    i = pl.program_id(0)
    n = pl.num_programs(0)

    def issue(ra_ref, rb_ref, slot):
        def body(r, carry):
            _row_gather_copy(y_hbm, buf.at[slot, 0], sem.at[slot], ra_ref[0, 0, r], r).start()
            _row_gather_copy(y_hbm, buf.at[slot, 1], sem.at[slot], rb_ref[0, 0, r], r).start()
            return carry
        lax.fori_loop(0, tm, body, 0, unroll=8)

    @pl.when(i == 0)
    def _():
        issue(r1_ref, r2_ref, 0)

    slot = lax.rem(i, 2)

    @pl.when(i + 1 < n)
    def _():
        issue(r1n_ref, r2n_ref, 1 - slot)

    pltpu.make_async_copy(y_hbm.at[pl.ds(0, tm), :], buf.at[slot, 0], sem.at[slot]).wait()
    pltpu.make_async_copy(y_hbm.at[pl.ds(0, tm), :], buf.at[slot, 1], sem.at[slot]).wait()
    gate = gate_ref[...]
    y = x_ref[...] + gate[:, 0:1] * buf[slot, 0] + gate[:, 1:2] * buf[slot, 1]
    out_ref[...] = _rms(y, g_ref[...])


def _moe_combine(x3, y_sorted, rows, gates, g):
    s = x3.shape[0]
    tm = TOKEN_TILE
    nt = s // tm
    r1 = rows[0].reshape(nt, 1, tm)
    r2 = rows[1].reshape(nt, 1, tm)
    cur = pl.BlockSpec((1, 1, tm), lambda i: (i, 0, 0), memory_space=pltpu.SMEM)
    nxt = pl.BlockSpec((1, 1, tm), lambda i: (jnp.minimum(i + 1, nt - 1), 0, 0), memory_space=pltpu.SMEM)
    return pl.pallas_call(
        _combine_kernel,
        grid=(nt,),
        in_specs=[cur, cur, nxt, nxt,
                  pl.BlockSpec((tm, D_MODEL), lambda i: (i, 0)),
                  pl.BlockSpec((tm, TOP_K), lambda i: (i, 0)),
                  _const_spec((1, D_MODEL)),
                  pl.BlockSpec(memory_space=pl.ANY)],
        out_specs=pl.BlockSpec((tm, D_MODEL), lambda i: (i, 0)),
        out_shape=jax.ShapeDtypeStruct((s, D_MODEL), F32),
        scratch_shapes=[pltpu.VMEM((2, TOP_K, tm, D_MODEL), F32), pltpu.SemaphoreType.DMA((2,))],
        compiler_params=_params(1),
        name="moe_combine",
    )(r1, r2, r1, r2, x3, gates, g.reshape(1, -1), y_sorted)


def _route(idx, s):
    tr = MOE_ROW_TILE
    n_rows = TOP_K * s + N_EXPERTS * tr
    n_tiles = n_rows // tr
    e_flat = idx[:TOP_K].reshape(-1)
    onehot = (e_flat[:, None] == jnp.arange(N_EXPERTS, dtype=I32)[None, :]).astype(I32)
    rank = jnp.sum((jnp.cumsum(onehot, axis=0) - onehot) * onehot, axis=1)
    counts = jnp.sum(onehot, axis=0)
    tiles_per = (counts + tr - 1) // tr
    tile_end = jnp.cumsum(tiles_per)
    row_start = (tile_end - tiles_per) * tr
    dest = row_start[e_flat] + rank
    tok = jnp.tile(jnp.arange(s, dtype=I32), TOP_K)
    tok_sorted = jnp.zeros((n_rows,), I32).at[dest].set(tok)
    n_used = tile_end[-1]
    tile_ids = jnp.arange(n_tiles, dtype=I32)
    tile_expert = jnp.sum((tile_ids[:, None] >= tile_end[None, :]).astype(I32), axis=1)
    last_expert = jnp.sum((n_used - 1 >= tile_end).astype(I32))
    tile_expert = jnp.where(tile_ids < n_used, tile_expert, last_expert).astype(I32)
    return tok_sorted, tile_expert, n_used.reshape(1).astype(I32), dest.reshape(TOP_K, s).astype(I32)


def kernel(x, positions, norm_mix, norm_ffn, final_norm, mla_w_in, mla_q_norm, mla_w_qb, mla_kv_norm, mla_w_kvb,
           mla_w_out, mlstm_w_in, mlstm_conv_w, mlstm_conv_b, mlstm_gate_b, mlstm_head_norm, mlstm_w_out,
           ffn_w_gate, ffn_w_up, ffn_w_down, moe_router, moe_w_gate, moe_w_up, moe_w_down):
    b, s, d = x.shape
    assert b == 1 and d == D_MODEL and s % TOKEN_TILE == 0 and s % ATTN_TILE == 0
    assert norm_mix.shape[0] == 2, "one attention layer followed by one mLSTM layer"
    x0 = x[0]
    pos = positions[0]
    q_t, k, v_t = _mla_proj(x0, pos, norm_mix[0], mla_w_in[0], mla_q_norm[0], mla_w_qb[0], mla_kv_norm[0],
                            mla_w_kvb[0])
    o = _mla_attn(q_t, k, v_t)
    x2 = _attn_out_ffn(x0, o, mla_w_out[0], norm_ffn[0], ffn_w_gate[0], ffn_w_up[0], ffn_w_down[0])
    q, kk, v, o_pre, gc, gt = _mlstm_proj(x2, norm_mix[1], mlstm_w_in[0], mlstm_conv_w[0], mlstm_conv_b[0],
                                          mlstm_gate_b[0])
    y = _mlstm_cell(q, kk, v, o_pre, gc, gt, mlstm_head_norm[0])
    x3, hn, idx, gates = _mlstm_out_router(x2, y, mlstm_w_out[0], norm_ffn[1], moe_router[0])
    tok_sorted, tile_expert, n_used, rows = _route(idx, s)
    y_sorted = _moe_experts(hn, tok_sorted, tile_expert, n_used, moe_w_gate[0], moe_w_up[0], moe_w_down[0])
    out = _moe_combine(x3, y_sorted, rows, gates[:TOP_K].T, final_norm)
    return out[None]
```

```python
import functools

import jax
import jax.numpy as jnp
from jax import lax
from jax.experimental import pallas as pl
from jax.experimental.pallas import tpu as pltpu

F32 = jnp.float32
BF16 = jnp.bfloat16
I32 = jnp.int32

D_MODEL = 1024
EPS = 1e-6
MLA_HEADS = 8
MLA_Q_LORA = 256
MLA_KV_LORA = 128
MLA_NOPE = 128
MLA_ROPE = 64
MLA_V = 128
MLA_QK = MLA_NOPE + MLA_ROPE
MLA_QK_PAD = 256
MLA_V_EXT = MLA_V + 16
ROPE_THETA = 10000.0
MLSTM_HEADS = 8
MLSTM_V = 128
MLSTM_QK = 64
MLSTM_QK_TOT = MLSTM_HEADS * MLSTM_QK
CONV_W = 4
GATE_CAP = 15.0
D_FF = 3584
N_EXPERTS = 8
TOP_K = 2

NEG = -1e30
LOG2_E = 1.4426950408889634

TOKEN_TILE = 512
ATTN_TILE = 512
ATTN_GROUP = 4
MLSTM_CHUNK = 256
MOE_ROW_TILE = 512
FF_CHUNK = 896
HALO = 8
VMEM_LIMIT = 56 * 1024 * 1024


def _dot(a, b):
    return jnp.dot(a, b, preferred_element_type=F32)


def _dot_nt(a, b):
    return lax.dot_general(a, b, (((1,), (1,)), ((), ())), preferred_element_type=F32)


def _dot_tn(a, b):
    return lax.dot_general(a, b, (((0,), (0,)), ((), ())), preferred_element_type=F32)


def _rms(x, g):
    return x * lax.rsqrt(jnp.mean(x * x, axis=-1, keepdims=True) + EPS) * g


def _silu(x):
    return x * jax.nn.sigmoid(x)


def _split3(x):
    hi = x.astype(BF16)
    r1 = x - hi.astype(F32)
    mid = r1.astype(BF16)
    lo = (r1 - mid.astype(F32)).astype(BF16)
    return hi, mid, lo


def _params(n_axes):
    return pltpu.CompilerParams(dimension_semantics=("arbitrary",) * n_axes, vmem_limit_bytes=VMEM_LIMIT)


def _const_spec(shape):
    return pl.BlockSpec(shape, lambda *_: (0,) * len(shape))


def _mla_proj_kernel(x_ref, posc_ref, posr_ref, g_ref, invr_ref, invc_ref, sgnr_ref, sgnc_ref, win_ref, qn_ref,
                     wqt_ref, kvn_ref, wk_ref, wvt_ref, qt_ref, k_ref, vt_ref):
    tm = x_ref.shape[0]
    xn = _rms(x_ref[...], g_ref[...]).astype(BF16)
    proj = _dot(xn, win_ref[...])
    cq = _rms(proj[:, :MLA_Q_LORA], qn_ref[...]).astype(BF16)
    ckv = _rms(proj[:, MLA_Q_LORA:MLA_Q_LORA + MLA_KV_LORA], kvn_ref[...]).astype(BF16)
    ang_r = posc_ref[...].astype(F32) * invr_ref[...]
    cos_r = jnp.cos(ang_r)
    sin_r = jnp.sin(ang_r) * sgnr_ref[...]
    ang_c = invc_ref[...] * posr_ref[...].astype(F32)
    cos_c = jnp.cos(ang_c)
    sin_c = jnp.sin(ang_c) * sgnc_ref[...]
    scale = MLA_QK ** -0.5 * LOG2_E
    n_nope = MLA_HEADS * MLA_NOPE
    n_rope = MLA_HEADS * MLA_ROPE
    q_t = _dot_nt(wqt_ref[...], cq)
    zq = jnp.zeros((MLA_QK_PAD - MLA_QK, tm), BF16)
    for h in range(MLA_HEADS):
        qt_ref[h, 0:MLA_NOPE, :] = (q_t[h * MLA_NOPE:(h + 1) * MLA_NOPE, :] * scale).astype(BF16)
        a = q_t[n_nope + h * MLA_ROPE:n_nope + (h + 1) * MLA_ROPE, :]
        b = q_t[n_nope + n_rope + h * MLA_ROPE:n_nope + n_rope + (h + 1) * MLA_ROPE, :]
        qt_ref[h, MLA_NOPE:MLA_QK, :] = ((a * cos_c + b * sin_c) * scale).astype(BF16)
        qt_ref[h, MLA_QK:MLA_QK_PAD, :] = zq
    k_nope = _dot(ckv, wk_ref[...])
    o_r = MLA_Q_LORA + MLA_KV_LORA
    k_rot = proj[:, o_r:o_r + MLA_ROPE] * cos_r + proj[:, o_r + MLA_ROPE:o_r + 2 * MLA_ROPE] * sin_r
    k_tail = jnp.concatenate([k_rot, jnp.zeros_like(k_rot)], axis=-1).astype(BF16)
    v_t = _dot_nt(wvt_ref[...], ckv)
    ones_rows = (lax.broadcasted_iota(I32, (MLA_V_EXT - MLA_V, tm), 0) == 0).astype(BF16)
    for h in range(MLA_HEADS):
        k_ref[h, :, 0:MLA_NOPE] = k_nope[:, h * MLA_NOPE:(h + 1) * MLA_NOPE].astype(BF16)
        k_ref[h, :, MLA_NOPE:MLA_QK_PAD] = k_tail
        vt_ref[h, 0, 0:MLA_V, :] = v_t[h * MLA_V:(h + 1) * MLA_V, :].astype(BF16)
        vt_ref[h, 0, MLA_V:MLA_V_EXT, :] = ones_rows


def _mla_proj(x, pos, g, w_in, q_norm, w_qb, kv_norm, w_kvb):
    s = x.shape[0]
    tm = ATTN_TILE
    nt = s // tm
    half = MLA_ROPE // 2
    o_r = MLA_Q_LORA + MLA_KV_LORA
    win = jnp.concatenate([w_in, w_in[:, o_r + half:], w_in[:, o_r:o_r + half]], axis=1).astype(BF16)
    wq = w_qb.reshape(MLA_Q_LORA, MLA_HEADS, MLA_QK)
    wq_rope = wq[:, :, MLA_NOPE:]
    wq_swap = jnp.concatenate([wq_rope[..., half:], wq_rope[..., :half]], axis=-1)
    wqt = jnp.concatenate([wq[:, :, :MLA_NOPE].reshape(MLA_Q_LORA, -1), wq_rope.reshape(MLA_Q_LORA, -1),
                           wq_swap.reshape(MLA_Q_LORA, -1)], axis=1).T.astype(BF16)
    wkv = w_kvb.reshape(MLA_KV_LORA, MLA_HEADS, MLA_NOPE + MLA_V)
    wk = wkv[:, :, :MLA_NOPE].reshape(MLA_KV_LORA, -1).astype(BF16)
    wvt = wkv[:, :, MLA_NOPE:].reshape(MLA_KV_LORA, -1).T.astype(BF16)
    inv = 1.0 / (ROPE_THETA ** (jnp.arange(0, MLA_ROPE, 2, dtype=F32) / MLA_ROPE))
    inv2 = jnp.concatenate([inv, inv])
    sgn = jnp.concatenate([-jnp.ones((half,), F32), jnp.ones((half,), F32)])
    n_q = wqt.shape[0]
    return pl.pallas_call(
        _mla_proj_kernel,
        grid=(nt,),
        in_specs=[
            pl.BlockSpec((tm, D_MODEL), lambda i: (i, 0)),
            pl.BlockSpec((tm, 1), lambda i: (i, 0)),
            pl.BlockSpec((1, tm), lambda i: (0, i)),
            _const_spec((1, D_MODEL)),
            _const_spec((1, MLA_ROPE)), _const_spec((MLA_ROPE, 1)),
            _const_spec((1, MLA_ROPE)), _const_spec((MLA_ROPE, 1)),
            _const_spec((D_MODEL, 512)),
            _const_spec((1, MLA_Q_LORA)),
            _const_spec((n_q, MLA_Q_LORA)),
            _const_spec((1, MLA_KV_LORA)),
            _const_spec((MLA_KV_LORA, MLA_HEADS * MLA_NOPE)),
            _const_spec((MLA_HEADS * MLA_V, MLA_KV_LORA)),
        ],
        out_specs=[
            pl.BlockSpec((MLA_HEADS, MLA_QK_PAD, tm), lambda i: (0, 0, i)),
            pl.BlockSpec((MLA_HEADS, tm, MLA_QK_PAD), lambda i: (0, i, 0)),
            pl.BlockSpec((MLA_HEADS, 1, MLA_V_EXT, tm), lambda i: (0, i, 0, 0)),
        ],
        out_shape=[
            jax.ShapeDtypeStruct((MLA_HEADS, MLA_QK_PAD, s), BF16),
            jax.ShapeDtypeStruct((MLA_HEADS, s, MLA_QK_PAD), BF16),
            jax.ShapeDtypeStruct((MLA_HEADS, nt, MLA_V_EXT, tm), BF16),
        ],
        compiler_params=_params(1),
        name="mla_proj",
    )(x, pos.reshape(s, 1), pos.reshape(1, s), g.reshape(1, -1), inv2.reshape(1, -1), inv2.reshape(-1, 1),
      sgn.reshape(1, -1), sgn.reshape(-1, 1), win, q_norm.reshape(1, -1), wqt, kv_norm.reshape(1, -1), wk, wvt)


def _attn_kernel(qt_ref, k_ref, vt_ref, o_ref, m_sc, acc_sc):
    t = ATTN_TILE
    g = pl.program_id(1)
    m_sc[...] = jnp.full(m_sc.shape, NEG, F32)
    acc_sc[...] = jnp.zeros(acc_sc.shape, F32)

    def scores(item):
        a, j, _ = item
        kj = k_ref[0, pl.ds(pl.multiple_of(j * t, t), t), :]
        return _dot(kj, qt_ref[0, :, a * t:(a + 1) * t])

    def update(item, s):
        a, j, diagonal = item
        if diagonal:
            kpos = lax.broadcasted_iota(I32, (t, t), 0)
            qpos = lax.broadcasted_iota(I32, (t, t), 1)
            s = jnp.where(kpos <= qpos, s, NEG)
        m_old = m_sc[a]
        m_new = jnp.maximum(m_old, jnp.max(s, axis=0, keepdims=True))
        p = jnp.exp2((s - m_new).astype(BF16))
        alpha = jnp.exp2(m_old - m_new)
        acc_sc[a] = alpha * acc_sc[a] + _dot(vt_ref[0, j], p)
        m_sc[a] = m_new

    def run(items):
        s_next = scores(items[0])
        for n, item in enumerate(items):
            s = s_next
            if n + 1 < len(items):
                s_next = scores(items[n + 1])
            update(item, s)

    def body(jj, carry):
        run([(a, 2 * jj + dj, False) for dj in range(2) for a in range(ATTN_GROUP)])
        return carry

    lax.fori_loop(0, (ATTN_GROUP // 2) * g, body, 0)
    j0 = ATTN_GROUP * g
    run([(a, j0 + d, a == d) for d in range(ATTN_GROUP) for a in range(d, ATTN_GROUP)])
    for a in range(ATTN_GROUP):
        o_t = acc_sc[a, 0:MLA_V, :] * (1.0 / acc_sc[a, MLA_V:MLA_V + 1, :])
        o_ref[a * t:(a + 1) * t, :] = o_t.T.astype(BF16)


def _mla_attn(q_t, k, v_t):
    s = k.shape[1]
    t = ATTN_TILE
    nt = s // t
    tg = ATTN_GROUP * t
    return pl.pallas_call(
        _attn_kernel,
        grid=(MLA_HEADS, s // tg),
        in_specs=[
            pl.BlockSpec((1, MLA_QK_PAD, tg), lambda h, g: (h, 0, g)),
            pl.BlockSpec((1, s, MLA_QK_PAD), lambda h, g: (h, 0, 0)),
            pl.BlockSpec((1, nt, MLA_V_EXT, t), lambda h, g: (h, 0, 0, 0)),
        ],
        out_specs=pl.BlockSpec((tg, MLA_V), lambda h, g: (g, h)),
        out_shape=jax.ShapeDtypeStruct((s, MLA_HEADS * MLA_V), BF16),
        scratch_shapes=[pltpu.VMEM((ATTN_GROUP, 1, t), F32), pltpu.VMEM((ATTN_GROUP, MLA_V_EXT, t), F32)],
        compiler_params=_params(2),
        name="mla_attn",
    )(q_t, k, v_t)


def _attn_out_ffn_kernel(x_ref, o_ref, wo_ref, g_ref, wg_ref, wu_ref, wd_ref, out_ref):
    x1 = x_ref[...] + _dot(o_ref[...], wo_ref[...])
    h = _rms(x1, g_ref[...]).astype(BF16)
    acc = x1
    for c in range(D_FF // FF_CHUNK):
        sl = slice(c * FF_CHUNK, (c + 1) * FF_CHUNK)
        a = (_silu(_dot(h, wg_ref[:, sl])) * _dot(h, wu_ref[:, sl])).astype(BF16)
        acc = acc + _dot(a, wd_ref[sl, :])
    out_ref[...] = acc


def _resident_spec(shape):
    return pl.BlockSpec(shape, lambda *_: (0,) * len(shape), pipeline_mode=pl.Buffered(1))


def _attn_out_ffn(x, o, w_out, g, wg, wu, wd):
    s = x.shape[0]
    tm = TOKEN_TILE
    return pl.pallas_call(
        _attn_out_ffn_kernel,
        grid=(s // tm,),
        in_specs=[
            pl.BlockSpec((tm, D_MODEL), lambda i: (i, 0)),
            pl.BlockSpec((tm, D_MODEL), lambda i: (i, 0)),
            _resident_spec((D_MODEL, D_MODEL)),
            _const_spec((1, D_MODEL)),
            _resident_spec((D_MODEL, D_FF)),
            _resident_spec((D_MODEL, D_FF)),
            _resident_spec((D_FF, D_MODEL)),
        ],
        out_specs=pl.BlockSpec((tm, D_MODEL), lambda i: (i, 0)),
        out_shape=jax.ShapeDtypeStruct((s, D_MODEL), F32),
        compiler_params=_params(1),
        name="attn_out_ffn",
    )(x, o, w_out.astype(BF16), g.reshape(1, -1), wg.astype(BF16), wu.astype(BF16), wd.astype(BF16))


def _gate_act(g, is_input_gate):
    g = GATE_CAP * jnp.tanh(g / GATE_CAP)
    log_f = jnp.minimum(g, 0.0) - jnp.log1p(jnp.exp(-jnp.abs(g)))
    return jnp.where(is_input_gate, g, log_f)


def _mlstm_proj_kernel(x_ref, g_ref, wqk_ref, wv_ref, wo_ref, wgc_ref, wgt_ref, cw_ref, cb_ref, gbc_ref, gbr_ref,
                       q_ref, k_ref, v_ref, op_ref, gc_ref, gt_ref, ext_sc):
    tm = x_ref.shape[0]

    @pl.when(pl.program_id(0) == 0)
    def _():
        ext_sc[0:HALO, :] = jnp.zeros((HALO, ext_sc.shape[1]), F32)

    xn = _rms(x_ref[...], g_ref[...]).astype(BF16)
    ext_sc[HALO:HALO + tm, :] = _dot(xn, wqk_ref[...])
    conv = cb_ref[...] + cw_ref[CONV_W - 1:CONV_W, :] * ext_sc[HALO:HALO + tm, :]
    for back in range(1, CONV_W):
        w_row = cw_ref[CONV_W - 1 - back:CONV_W - back, :]
        conv = conv + w_row * ext_sc[HALO - back:HALO - back + tm, :]
    ext_sc[0:HALO, :] = ext_sc[tm:tm + HALO, :]
    qk = _silu(conv)
    q_ref[...] = (qk[:, :MLSTM_QK_TOT] * (MLSTM_QK ** -0.5)).astype(BF16)
    k_ref[...] = qk[:, MLSTM_QK_TOT:].astype(BF16)
    v_ref[...] = _dot(xn, wv_ref[...]).astype(BF16)
    op_ref[...] = _dot(xn, wo_ref[...])
    gc = _dot(xn, wgc_ref[...]) + gbc_ref[...]
    gc_ref[...] = _gate_act(gc, lax.broadcasted_iota(I32, gc.shape, 1) < MLSTM_HEADS)
    gt = _dot_nt(wgt_ref[...], xn) + gbr_ref[...]
    gt_ref[...] = _gate_act(gt, lax.broadcasted_iota(I32, gt.shape, 0) < MLSTM_HEADS)


def _mlstm_proj(x, g, w_in, conv_w, conv_b, gate_b):
    s = x.shape[0]
    tm = TOKEN_TILE
    o1 = 2 * MLSTM_QK_TOT
    o2 = o1 + MLSTM_HEADS * MLSTM_V
    o3 = o2 + D_MODEL
    ng = 2 * MLSTM_HEADS
    w = w_in.astype(BF16)
    return pl.pallas_call(
        _mlstm_proj_kernel,
        grid=(s // tm,),
        in_specs=[
            pl.BlockSpec((tm, D_MODEL), lambda i: (i, 0)),
            _const_spec((1, D_MODEL)),
            _const_spec((D_MODEL, o1)), _const_spec((D_MODEL, o2 - o1)), _const_spec((D_MODEL, o3 - o2)),
            _const_spec((D_MODEL, ng)), _const_spec((ng, D_MODEL)),
            _const_spec((CONV_W, o1)), _const_spec((1, o1)),
            _const_spec((1, ng)), _const_spec((ng, 1)),
        ],
        out_specs=[
            pl.BlockSpec((tm, MLSTM_QK_TOT), lambda i: (i, 0)),
            pl.BlockSpec((tm, MLSTM_QK_TOT), lambda i: (i, 0)),
            pl.BlockSpec((tm, o2 - o1), lambda i: (i, 0)),
            pl.BlockSpec((tm, D_MODEL), lambda i: (i, 0)),
            pl.BlockSpec((tm, ng), lambda i: (i, 0)),
            pl.BlockSpec((ng, tm), lambda i: (0, i)),
        ],
        out_shape=[
            jax.ShapeDtypeStruct((s, MLSTM_QK_TOT), BF16),
            jax.ShapeDtypeStruct((s, MLSTM_QK_TOT), BF16),
            jax.ShapeDtypeStruct((s, o2 - o1), BF16),
            jax.ShapeDtypeStruct((s, D_MODEL), F32),
            jax.ShapeDtypeStruct((s, ng), F32),
            jax.ShapeDtypeStruct((ng, s), F32),
        ],
        scratch_shapes=[pltpu.VMEM((tm + HALO, o1), F32)],
        compiler_params=_params(1),
        name="mlstm_proj",
    )(x, g.reshape(1, -1), w[:, :o1], w[:, o1:o2], w[:, o2:o3], w[:, o3:], w[:, o3:].T, conv_w,
      conv_b.reshape(1, -1), gate_b.reshape(1, -1), gate_b.reshape(-1, 1))


def _mlstm_cell_kernel(q_ref, k_ref, v_ref, op_ref, gc_ref, gt_ref, hn_ref, tri_ref, y_ref, c_sc, m_sc):
    n = q_ref.shape[0]
    nh = MLSTM_HEADS

    @pl.when(pl.program_id(0) == 0)
    def _():
        c_sc[...] = jnp.zeros(c_sc.shape, F32)
        m_sc[...] = jnp.zeros(m_sc.shape, F32)

    gc = gc_ref[...]
    gt = gt_ref[...]
    ig_c, lf_c = gc[:, :nh], gc[:, nh:]
    ig_r, lf_r = gt[:nh, :], gt[nh:, :]
    tri = tri_ref[...]
    b_c = sum(_dot(tri, part) for part in _split3(lf_c))
    b_r = sum(_dot_nt(part, tri) for part in _split3(lf_r))
    b_last = b_c[n - 1:n, :]
    m_prev = m_sc[...]
    a_c = b_last - b_c + ig_c
    m_loc = jnp.max(a_c, axis=0, keepdims=True)
    w_c = jnp.exp(a_c - m_loc)
    m_new = jnp.maximum(b_last + m_prev, m_loc)
    s_prev = jnp.exp(b_last + m_prev - m_new)
    s_loc = jnp.exp(m_loc - m_new)
    inter_c = b_c + m_prev
    r_r = ig_r - b_r
    causal = lax.broadcasted_iota(I32, (n, n), 1) <= lax.broadcasted_iota(I32, (n, n), 0)
    ones_col = (lax.broadcasted_iota(I32, (n, MLSTM_V), 1) == 0).astype(BF16)
    for h in range(nh):
        qh = q_ref[:, h * MLSTM_QK:(h + 1) * MLSTM_QK]
        kh = k_ref[:, h * MLSTM_QK:(h + 1) * MLSTM_QK]
        v_ext = jnp.concatenate([v_ref[:, h * MLSTM_V:(h + 1) * MLSTM_V], ones_col], axis=-1)
        d = jnp.where(causal, b_c[:, h:h + 1] + r_r[h:h + 1, :], NEG)
        inter = inter_c[:, h:h + 1]
        m_t = jnp.maximum(inter, jnp.max(d, axis=1, keepdims=True))
        wts = (jnp.exp(d - m_t) * _dot_nt(qh, kh)).astype(BF16)
        c_prev = c_sc[h]
        num = _dot(wts, v_ext) + jnp.exp(inter - m_t) * _dot(qh, c_prev.astype(BF16))
        den = num[:, MLSTM_V:MLSTM_V + 1]
        hh = num[:, :MLSTM_V] / jnp.maximum(jnp.abs(den), jnp.exp(-m_t))
        hh = _rms(hh, hn_ref[:, h * MLSTM_V:(h + 1) * MLSTM_V])
        y = jax.nn.sigmoid(op_ref[:, h * MLSTM_V:(h + 1) * MLSTM_V]) * hh
        y_ref[:, h * MLSTM_V:(h + 1) * MLSTM_V] = y.astype(BF16)
        kw = (kh.astype(F32) * w_c[:, h:h + 1]).astype(BF16)
        c_sc[h] = s_prev[:, h:h + 1] * c_prev + s_loc[:, h:h + 1] * _dot_tn(kw, v_ext)
    m_sc[...] = m_new


def _mlstm_cell(q, k, v, o_pre, gc, gt, head_norm):
    s = q.shape[0]
    n = MLSTM_CHUNK
    ng = 2 * MLSTM_HEADS
    tri = jnp.tril(jnp.ones((n, n), BF16))
    row = lambda w: pl.BlockSpec((n, w), lambda c: (c, 0))
    return pl.pallas_call(
        _mlstm_cell_kernel,
        grid=(s // n,),
        in_specs=[row(MLSTM_QK_TOT), row(MLSTM_QK_TOT), row(D_MODEL), row(D_MODEL), row(ng),
                  pl.BlockSpec((ng, n), lambda c: (0, c)), _const_spec((1, D_MODEL)), _const_spec((n, n))],
        out_specs=row(D_MODEL),
        out_shape=jax.ShapeDtypeStruct((s, D_MODEL), BF16),
        scratch_shapes=[pltpu.VMEM((MLSTM_HEADS, MLSTM_QK, 2 * MLSTM_V), F32), pltpu.VMEM((1, MLSTM_HEADS), F32)],
        compiler_params=_params(1),
        name="mlstm_cell",
    )(q, k, v, o_pre, gc, gt, head_norm.reshape(1, -1), tri)


def _mlstm_out_router_kernel(x_ref, y_ref, wo_ref, g_ref, rt_ref, x3_ref, hn_ref, idx_ref, gate_ref):
    x3 = x_ref[...] + _dot(y_ref[...], wo_ref[...])
    hn = _rms(x3, g_ref[...])
    x3_ref[...] = x3
    hn_ref[...] = hn
    logits = lax.dot_general(rt_ref[...], hn, (((1,), (1,)), ((), ())), precision=lax.Precision.HIGHEST,
                             preferred_element_type=F32)
    e_iota = lax.broadcasted_iota(I32, logits.shape, 0)
    m1 = jnp.max(logits, axis=0, keepdims=True)
    i1 = jnp.min(jnp.where(logits == m1, e_iota, N_EXPERTS), axis=0, keepdims=True)
    rest = jnp.where(e_iota == i1, -jnp.inf, logits)
    m2 = jnp.max(rest, axis=0, keepdims=True)
    i2 = jnp.min(jnp.where(rest == m2, e_iota, N_EXPERTS), axis=0, keepdims=True)
    e2 = jnp.exp(m2 - m1)
    g1 = 1.0 / (1.0 + e2)
    idx_ref[...] = jnp.where(e_iota == 0, i1, jnp.where(e_iota == 1, i2, 0))
    gate_ref[...] = jnp.where(e_iota == 0, g1, jnp.where(e_iota == 1, e2 * g1, 0.0))


def _mlstm_out_router(x, y, w_out, g, router):
    s = x.shape[0]
    tm = TOKEN_TILE
    tok = pl.BlockSpec((tm, D_MODEL), lambda i: (i, 0))
    lane = pl.BlockSpec((N_EXPERTS, tm), lambda i: (0, i))
    return pl.pallas_call(
        _mlstm_out_router_kernel,
        grid=(s // tm,),
        in_specs=[tok, tok, _const_spec((D_MODEL, D_MODEL)), _const_spec((1, D_MODEL)),
                  _const_spec((N_EXPERTS, D_MODEL))],
        out_specs=[tok, tok, lane, lane],
        out_shape=[jax.ShapeDtypeStruct((s, D_MODEL), F32), jax.ShapeDtypeStruct((s, D_MODEL), F32),
                   jax.ShapeDtypeStruct((N_EXPERTS, s), I32), jax.ShapeDtypeStruct((N_EXPERTS, s), F32)],
        compiler_params=_params(1),
        name="mlstm_out_router",
    )(x, y, w_out.astype(BF16), g.reshape(1, -1), router.T)


def _row_gather_copy(src_hbm, dst, sem, src_row, dst_row):
    return pltpu.make_async_copy(src_hbm.at[pl.ds(src_row, 1), :], dst.at[pl.ds(dst_row, 1), :], sem)


def _moe_kernel(te_ref, nu_ref, tok_ref, tok_next_ref, hn_hbm, wg_ref, wu_ref, wd_ref, y_ref, xf_sc, xb_sc, sem):
    del te_ref
    tr = xf_sc.shape[0]
    t = pl.program_id(0)
    c = pl.program_id(1)
    n_used = nu_ref[0]

    def issue(idx_ref):
        def body(r, carry):
            _row_gather_copy(hn_hbm, xf_sc, sem, idx_ref[0, 0, r], r).start()
            return carry
        lax.fori_loop(0, tr, body, 0, unroll=8)

    @pl.when((c == 0) & (t == 0) & (n_used > 0))
    def _():
        issue(tok_ref)

    @pl.when((c == 0) & (t < n_used))
    def _():
        pltpu.make_async_copy(hn_hbm.at[pl.ds(0, tr), :], xf_sc, sem).wait()
        xb_sc[...] = xf_sc[...].astype(BF16)

    @pl.when((c == 0) & (t + 1 < n_used))
    def _():
        issue(tok_next_ref)

    @pl.when(c == 0)
    def _():
        y_ref[...] = jnp.zeros(y_ref.shape, F32)

    @pl.when(t < n_used)
    def _():
        xb = xb_sc[...]
        a = (_silu(_dot(xb, wg_ref[0])) * _dot(xb, wu_ref[0])).astype(BF16)
        y_ref[...] += _dot(a, wd_ref[0])


def _moe_experts(hn, tok_sorted, tile_expert, n_used, wg, wu, wd):
    r = tok_sorted.shape[0]
    tr = MOE_ROW_TILE
    nt = r // tr
    nc = D_FF // FF_CHUNK
    tok3 = tok_sorted.reshape(nt, 1, tr)
    grid_spec = pltpu.PrefetchScalarGridSpec(
        num_scalar_prefetch=2,
        grid=(nt, nc),
        in_specs=[
            pl.BlockSpec((1, 1, tr), lambda t, c, te, nu: (t, 0, 0), memory_space=pltpu.SMEM),
            pl.BlockSpec((1, 1, tr), lambda t, c, te, nu: (jnp.minimum(t + 1, nt - 1), 0, 0),
                         memory_space=pltpu.SMEM),
            pl.BlockSpec(memory_space=pl.ANY),
            pl.BlockSpec((1, D_MODEL, FF_CHUNK), lambda t, c, te, nu: (te[t], 0, c)),
            pl.BlockSpec((1, D_MODEL, FF_CHUNK), lambda t, c, te, nu: (te[t], 0, c)),
            pl.BlockSpec((1, FF_CHUNK, D_MODEL), lambda t, c, te, nu: (te[t], c, 0)),
        ],
        out_specs=pl.BlockSpec((tr, D_MODEL), lambda t, c, te, nu: (t, 0)),
        scratch_shapes=[pltpu.VMEM((tr, D_MODEL), F32), pltpu.VMEM((tr, D_MODEL), BF16), pltpu.SemaphoreType.DMA],
    )
    return pl.pallas_call(
        _moe_kernel,
        grid_spec=grid_spec,
        out_shape=jax.ShapeDtypeStruct((r, D_MODEL), F32),
        compiler_params=_params(2),
        name="moe_experts",
    )(tile_expert, n_used, tok3, tok3, hn, wg.astype(BF16), wu.astype(BF16), wd.astype(BF16))


def _combine_kernel(r1_ref, r2_ref, r1n_ref, r2n_ref, x_ref, gate_ref, g_ref, y_hbm, out_ref, buf, sem):
    tm = x_ref.shape[0]
    i = pl.program_id(0)
    n = pl.num_programs(0)

    def issue(ra_ref, rb_ref, slot):
        def body(r, carry):
            _row_gather_copy(y_hbm, buf.at[slot, 0], sem.at[slot], ra_ref[0, 0, r], r).start()
            _row_gather_copy(y_hbm, buf.at[slot, 1], sem.at[slot], rb_ref[0, 0, r], r).start()
            return carry
        lax.fori_loop(0, tm, body, 0, unroll=8)

    @pl.when(i == 0)
    def _():
        issue(r1_ref, r2_ref, 0)

    slot = lax.rem(i, 2)

    @pl.when(i + 1 < n)
    def _():
        issue(r1n_ref, r2n_ref, 1 - slot)

    pltpu.make_async_copy(y_hbm.at[pl.ds(0, tm), :], buf.at[slot, 0], sem.at[slot]).wait()
    pltpu.make_async_copy(y_hbm.at[pl.ds(0, tm), :], buf.at[slot, 1], sem.at[slot]).wait()
    gate = gate_ref[...]
    y = x_ref[...] + gate[:, 0:1] * buf[slot, 0] + gate[:, 1:2] * buf[slot, 1]
    out_ref[...] = _rms(y, g_ref[...])


def _moe_combine(x3, y_sorted, rows, gates, g):
    s = x3.shape[0]
    tm = TOKEN_TILE
    nt = s // tm
    r1 = rows[0].reshape(nt, 1, tm)
    r2 = rows[1].reshape(nt, 1, tm)
    cur = pl.BlockSpec((1, 1, tm), lambda i: (i, 0, 0), memory_space=pltpu.SMEM)
    nxt = pl.BlockSpec((1, 1, tm), lambda i: (jnp.minimum(i + 1, nt - 1), 0, 0), memory_space=pltpu.SMEM)
    return pl.pallas_call(
        _combine_kernel,
        grid=(nt,),
        in_specs=[cur, cur, nxt, nxt,
                  pl.BlockSpec((tm, D_MODEL), lambda i: (i, 0)),
                  pl.BlockSpec((tm, TOP_K), lambda i: (i, 0)),
                  _const_spec((1, D_MODEL)),
                  pl.BlockSpec(memory_space=pl.ANY)],
        out_specs=pl.BlockSpec((tm, D_MODEL), lambda i: (i, 0)),
        out_shape=jax.ShapeDtypeStruct((s, D_MODEL), F32),
        scratch_shapes=[pltpu.VMEM((2, TOP_K, tm, D_MODEL), F32), pltpu.SemaphoreType.DMA((2,))],
        compiler_params=_params(1),
        name="moe_combine",
    )(r1, r2, r1, r2, x3, gates, g.reshape(1, -1), y_sorted)


def _route(idx, s):
    tr = MOE_ROW_TILE
    n_rows = TOP_K * s + N_EXPERTS * tr
    n_tiles = n_rows // tr
    e_flat = idx[:TOP_K].reshape(-1)
    onehot = (e_flat[:, None] == jnp.arange(N_EXPERTS, dtype=I32)[None, :]).astype(I32)
    rank = jnp.sum((jnp.cumsum(onehot, axis=0) - onehot) * onehot, axis=1)
    counts = jnp.sum(onehot, axis=0)
    tiles_per = (counts + tr - 1) // tr
    tile_end = jnp.cumsum(tiles_per)
    row_start = (tile_end - tiles_per) * tr
    dest = row_start[e_flat] + rank
    tok = jnp.tile(jnp.arange(s, dtype=I32), TOP_K)
    tok_sorted = jnp.zeros((n_rows,), I32).at[dest].set(tok)
    n_used = tile_end[-1]
    tile_ids = jnp.arange(n_tiles, dtype=I32)
    tile_expert = jnp.sum((tile_ids[:, None] >= tile_end[None, :]).astype(I32), axis=1)
    last_expert = jnp.sum((n_used - 1 >= tile_end).astype(I32))
    tile_expert = jnp.where(tile_ids < n_used, tile_expert, last_expert).astype(I32)
    return tok_sorted, tile_expert, n_used.reshape(1).astype(I32), dest.reshape(TOP_K, s).astype(I32)


def kernel(x, positions, norm_mix, norm_ffn, final_norm, mla_w_in, mla_q_norm, mla_w_qb, mla_kv_norm, mla_w_kvb,
           mla_w_out, mlstm_w_in, mlstm_conv_w, mlstm_conv_b, mlstm_gate_b, mlstm_head_norm, mlstm_w_out,
           ffn_w_gate, ffn_w_up, ffn_w_down, moe_router, moe_w_gate, moe_w_up, moe_w_down):
    b, s, d = x.shape
    assert b == 1 and d == D_MODEL and s % TOKEN_TILE == 0 and s % (ATTN_GROUP * ATTN_TILE) == 0
    assert norm_mix.shape[0] == 2, "one attention layer followed by one mLSTM layer"
    x0 = x[0]
    pos = positions[0]
    q_t, k, v_t = _mla_proj(x0, pos, norm_mix[0], mla_w_in[0], mla_q_norm[0], mla_w_qb[0], mla_kv_norm[0],
                            mla_w_kvb[0])
    o = _mla_attn(q_t, k, v_t)
    x2 = _attn_out_ffn(x0, o, mla_w_out[0], norm_ffn[0], ffn_w_gate[0], ffn_w_up[0], ffn_w_down[0])
    q, kk, v, o_pre, gc, gt = _mlstm_proj(x2, norm_mix[1], mlstm_w_in[0], mlstm_conv_w[0], mlstm_conv_b[0],
                                          mlstm_gate_b[0])
    y = _mlstm_cell(q, kk, v, o_pre, gc, gt, mlstm_head_norm[0])
    x3, hn, idx, gates = _mlstm_out_router(x2, y, mlstm_w_out[0], norm_ffn[1], moe_router[0])
    tok_sorted, tile_expert, n_used, rows = _route(idx, s)
    y_sorted = _moe_experts(hn, tok_sorted, tile_expert, n_used, moe_w_gate[0], moe_w_up[0], moe_w_down[0])
    out = _moe_combine(x3, y_sorted, rows, gates[:TOP_K].T, final_norm)
    return out[None]
```

```python
import functools

import jax
import jax.numpy as jnp
from jax import lax
from jax.experimental import pallas as pl
from jax.experimental.pallas import tpu as pltpu

F32 = jnp.float32
BF16 = jnp.bfloat16
I32 = jnp.int32

D_MODEL = 1024
EPS = 1e-6
MLA_HEADS = 8
MLA_Q_LORA = 256
MLA_KV_LORA = 128
MLA_NOPE = 128
MLA_ROPE = 64
MLA_V = 128
MLA_QK = MLA_NOPE + MLA_ROPE
MLA_QK_PAD = 256
MLA_V_EXT = MLA_V + 16
ROPE_THETA = 10000.0
MLSTM_HEADS = 8
MLSTM_V = 128
MLSTM_QK = 64
MLSTM_V_EXT = MLSTM_V + 16
MLSTM_QK_TOT = MLSTM_HEADS * MLSTM_QK
CONV_W = 4
GATE_CAP = 15.0
D_FF = 3584
N_EXPERTS = 8
TOP_K = 2

NEG = -1e30
LOG2_E = 1.4426950408889634

TOKEN_TILE = 512
ATTN_TILE = 512
ATTN_GROUP = 4
MLSTM_CHUNK = 256
MOE_ROW_TILE = 512
FF_CHUNK = 896
HALO = 8
VMEM_LIMIT = 56 * 1024 * 1024


def _dot(a, b):
    return jnp.dot(a, b, preferred_element_type=F32)


def _dot_nt(a, b):
    return lax.dot_general(a, b, (((1,), (1,)), ((), ())), preferred_element_type=F32)


def _dot_tn(a, b):
    return lax.dot_general(a, b, (((0,), (0,)), ((), ())), preferred_element_type=F32)


def _rms(x, g):
    return x * lax.rsqrt(jnp.mean(x * x, axis=-1, keepdims=True) + EPS) * g


def _silu(x):
    return x * jax.nn.sigmoid(x)


def _split3(x):
    hi = x.astype(BF16)
    r1 = x - hi.astype(F32)
    mid = r1.astype(BF16)
    lo = (r1 - mid.astype(F32)).astype(BF16)
    return hi, mid, lo


def _params(n_axes):
    return pltpu.CompilerParams(dimension_semantics=("arbitrary",) * n_axes, vmem_limit_bytes=VMEM_LIMIT)


def _const_spec(shape):
    return pl.BlockSpec(shape, lambda *_: (0,) * len(shape))


def _mla_proj_kernel(x_ref, posc_ref, posr_ref, g_ref, invr_ref, invc_ref, sgnr_ref, sgnc_ref, win_ref, qn_ref,
                     wqt_ref, kvn_ref, wk_ref, wvt_ref, qt_ref, k_ref, vt_ref):
    tm = x_ref.shape[0]
    xn = _rms(x_ref[...], g_ref[...]).astype(BF16)
    proj = _dot(xn, win_ref[...])
    cq = _rms(proj[:, :MLA_Q_LORA], qn_ref[...]).astype(BF16)
    ckv = _rms(proj[:, MLA_Q_LORA:MLA_Q_LORA + MLA_KV_LORA], kvn_ref[...]).astype(BF16)
    ang_r = posc_ref[...].astype(F32) * invr_ref[...]
    cos_r = jnp.cos(ang_r)
    sin_r = jnp.sin(ang_r) * sgnr_ref[...]
    ang_c = invc_ref[...] * posr_ref[...].astype(F32)
    cos_c = jnp.cos(ang_c)
    sin_c = jnp.sin(ang_c) * sgnc_ref[...]
    scale = MLA_QK ** -0.5 * LOG2_E
    n_nope = MLA_HEADS * MLA_NOPE
    n_rope = MLA_HEADS * MLA_ROPE
    q_t = _dot_nt(wqt_ref[...], cq)
    zq = jnp.zeros((MLA_QK_PAD - MLA_QK, tm), BF16)
    for h in range(MLA_HEADS):
        qt_ref[h, 0:MLA_NOPE, :] = (q_t[h * MLA_NOPE:(h + 1) * MLA_NOPE, :] * scale).astype(BF16)
        a = q_t[n_nope + h * MLA_ROPE:n_nope + (h + 1) * MLA_ROPE, :]
        b = q_t[n_nope + n_rope + h * MLA_ROPE:n_nope + n_rope + (h + 1) * MLA_ROPE, :]
        qt_ref[h, MLA_NOPE:MLA_QK, :] = ((a * cos_c + b * sin_c) * scale).astype(BF16)
        qt_ref[h, MLA_QK:MLA_QK_PAD, :] = zq
    k_nope = _dot(ckv, wk_ref[...])
    o_r = MLA_Q_LORA + MLA_KV_LORA
    k_rot = proj[:, o_r:o_r + MLA_ROPE] * cos_r + proj[:, o_r + MLA_ROPE:o_r + 2 * MLA_ROPE] * sin_r
    k_tail = jnp.concatenate([k_rot, jnp.zeros_like(k_rot)], axis=-1).astype(BF16)
    v_t = _dot_nt(wvt_ref[...], ckv)
    ones_rows = (lax.broadcasted_iota(I32, (MLA_V_EXT - MLA_V, tm), 0) == 0).astype(BF16)
    for h in range(MLA_HEADS):
        k_ref[h, :, 0:MLA_NOPE] = k_nope[:, h * MLA_NOPE:(h + 1) * MLA_NOPE].astype(BF16)
        k_ref[h, :, MLA_NOPE:MLA_QK_PAD] = k_tail
        vt_ref[h, 0, 0:MLA_V, :] = v_t[h * MLA_V:(h + 1) * MLA_V, :].astype(BF16)
        vt_ref[h, 0, MLA_V:MLA_V_EXT, :] = ones_rows


def _mla_proj(x, pos, g, w_in, q_norm, w_qb, kv_norm, w_kvb):
    s = x.shape[0]
    tm = ATTN_TILE
    nt = s // tm
    half = MLA_ROPE // 2
    o_r = MLA_Q_LORA + MLA_KV_LORA
    win = jnp.concatenate([w_in, w_in[:, o_r + half:], w_in[:, o_r:o_r + half]], axis=1).astype(BF16)
    wq = w_qb.reshape(MLA_Q_LORA, MLA_HEADS, MLA_QK)
    wq_rope = wq[:, :, MLA_NOPE:]
    wq_swap = jnp.concatenate([wq_rope[..., half:], wq_rope[..., :half]], axis=-1)
    wqt = jnp.concatenate([wq[:, :, :MLA_NOPE].reshape(MLA_Q_LORA, -1), wq_rope.reshape(MLA_Q_LORA, -1),
                           wq_swap.reshape(MLA_Q_LORA, -1)], axis=1).T.astype(BF16)
    wkv = w_kvb.reshape(MLA_KV_LORA, MLA_HEADS, MLA_NOPE + MLA_V)
    wk = wkv[:, :, :MLA_NOPE].reshape(MLA_KV_LORA, -1).astype(BF16)
    wvt = wkv[:, :, MLA_NOPE:].reshape(MLA_KV_LORA, -1).T.astype(BF16)
    inv = 1.0 / (ROPE_THETA ** (jnp.arange(0, MLA_ROPE, 2, dtype=F32) / MLA_ROPE))
    inv2 = jnp.concatenate([inv, inv])
    sgn = jnp.concatenate([-jnp.ones((half,), F32), jnp.ones((half,), F32)])
    n_q = wqt.shape[0]
    return pl.pallas_call(
        _mla_proj_kernel,
        grid=(nt,),
        in_specs=[
            pl.BlockSpec((tm, D_MODEL), lambda i: (i, 0)),
            pl.BlockSpec((tm, 1), lambda i: (i, 0)),
            pl.BlockSpec((1, tm), lambda i: (0, i)),
            _const_spec((1, D_MODEL)),
            _const_spec((1, MLA_ROPE)), _const_spec((MLA_ROPE, 1)),
            _const_spec((1, MLA_ROPE)), _const_spec((MLA_ROPE, 1)),
            _const_spec((D_MODEL, 512)),
            _const_spec((1, MLA_Q_LORA)),
            _const_spec((n_q, MLA_Q_LORA)),
            _const_spec((1, MLA_KV_LORA)),
            _const_spec((MLA_KV_LORA, MLA_HEADS * MLA_NOPE)),
            _const_spec((MLA_HEADS * MLA_V, MLA_KV_LORA)),
        ],
        out_specs=[
            pl.BlockSpec((MLA_HEADS, MLA_QK_PAD, tm), lambda i: (0, 0, i)),
            pl.BlockSpec((MLA_HEADS, tm, MLA_QK_PAD), lambda i: (0, i, 0)),
            pl.BlockSpec((MLA_HEADS, 1, MLA_V_EXT, tm), lambda i: (0, i, 0, 0)),
        ],
        out_shape=[
            jax.ShapeDtypeStruct((MLA_HEADS, MLA_QK_PAD, s), BF16),
            jax.ShapeDtypeStruct((MLA_HEADS, s, MLA_QK_PAD), BF16),
            jax.ShapeDtypeStruct((MLA_HEADS, nt, MLA_V_EXT, tm), BF16),
        ],
        compiler_params=_params(1),
        name="mla_proj",
    )(x, pos.reshape(s, 1), pos.reshape(1, s), g.reshape(1, -1), inv2.reshape(1, -1), inv2.reshape(-1, 1),
      sgn.reshape(1, -1), sgn.reshape(-1, 1), win, q_norm.reshape(1, -1), wqt, kv_norm.reshape(1, -1), wk, wvt)


def _attn_kernel(qt_ref, k_ref, vt_ref, o_ref, m_sc, acc_sc):
    t = ATTN_TILE
    g = pl.program_id(1)
    m_sc[...] = jnp.full(m_sc.shape, NEG, F32)
    acc_sc[...] = jnp.zeros(acc_sc.shape, F32)

    def scores(item):
        a, j, _ = item
        kj = k_ref[0, pl.ds(pl.multiple_of(j * t, t), t), :]
        return _dot(kj, qt_ref[0, :, a * t:(a + 1) * t])

    def update(item, s):
        a, j, diagonal = item
        if diagonal:
            kpos = lax.broadcasted_iota(I32, (t, t), 0)
            qpos = lax.broadcasted_iota(I32, (t, t), 1)
            s = jnp.where(kpos <= qpos, s, NEG)
        m_old = m_sc[a]
        m_new = jnp.maximum(m_old, jnp.max(s, axis=0, keepdims=True))
        p = jnp.exp2((s - m_new).astype(BF16))
        alpha = jnp.exp2(m_old - m_new)
        acc_sc[a] = alpha * acc_sc[a] + _dot(vt_ref[0, j], p)
        m_sc[a] = m_new

    def run(items):
        s_next = scores(items[0])
        for n, item in enumerate(items):
            s = s_next
            if n + 1 < len(items):
                s_next = scores(items[n + 1])
            update(item, s)

    def body(jj, carry):
        run([(a, 2 * jj + dj, False) for dj in range(2) for a in range(ATTN_GROUP)])
        return carry

    lax.fori_loop(0, (ATTN_GROUP // 2) * g, body, 0)
    j0 = ATTN_GROUP * g
    run([(a, j0 + d, a == d) for d in range(ATTN_GROUP) for a in range(d, ATTN_GROUP)])
    for a in range(ATTN_GROUP):
        o_t = acc_sc[a, 0:MLA_V, :] * (1.0 / acc_sc[a, MLA_V:MLA_V + 1, :])
        o_ref[a * t:(a + 1) * t, :] = o_t.T.astype(BF16)


def _mla_attn(q_t, k, v_t):
    s = k.shape[1]
    t = ATTN_TILE
    nt = s // t
    tg = ATTN_GROUP * t
    return pl.pallas_call(
        _attn_kernel,
        grid=(MLA_HEADS, s // tg),
        in_specs=[
            pl.BlockSpec((1, MLA_QK_PAD, tg), lambda h, g: (h, 0, g)),
            pl.BlockSpec((1, s, MLA_QK_PAD), lambda h, g: (h, 0, 0)),
            pl.BlockSpec((1, nt, MLA_V_EXT, t), lambda h, g: (h, 0, 0, 0)),
        ],
        out_specs=pl.BlockSpec((tg, MLA_V), lambda h, g: (g, h)),
        out_shape=jax.ShapeDtypeStruct((s, MLA_HEADS * MLA_V), BF16),
        scratch_shapes=[pltpu.VMEM((ATTN_GROUP, 1, t), F32), pltpu.VMEM((ATTN_GROUP, MLA_V_EXT, t), F32)],
        compiler_params=_params(2),
        name="mla_attn",
    )(q_t, k, v_t)


def _attn_out_ffn_kernel(x_ref, o_ref, wo_ref, g_ref, wg_ref, wu_ref, wd_ref, out_ref):
    x1 = x_ref[...] + _dot(o_ref[...], wo_ref[...])
    h = _rms(x1, g_ref[...]).astype(BF16)
    acc = x1
    for c in range(D_FF // FF_CHUNK):
        sl = slice(c * FF_CHUNK, (c + 1) * FF_CHUNK)
        a = (_silu(_dot(h, wg_ref[:, sl])) * _dot(h, wu_ref[:, sl])).astype(BF16)
        acc = acc + _dot(a, wd_ref[sl, :])
    out_ref[...] = acc


def _resident_spec(shape):
    return pl.BlockSpec(shape, lambda *_: (0,) * len(shape), pipeline_mode=pl.Buffered(1))


def _attn_out_ffn(x, o, w_out, g, wg, wu, wd):
    s = x.shape[0]
    tm = TOKEN_TILE
    return pl.pallas_call(
        _attn_out_ffn_kernel,
        grid=(s // tm,),
        in_specs=[
            pl.BlockSpec((tm, D_MODEL), lambda i: (i, 0)),
            pl.BlockSpec((tm, D_MODEL), lambda i: (i, 0)),
            _resident_spec((D_MODEL, D_MODEL)),
            _const_spec((1, D_MODEL)),
            _resident_spec((D_MODEL, D_FF)),
            _resident_spec((D_MODEL, D_FF)),
            _resident_spec((D_FF, D_MODEL)),
        ],
        out_specs=pl.BlockSpec((tm, D_MODEL), lambda i: (i, 0)),
        out_shape=jax.ShapeDtypeStruct((s, D_MODEL), F32),
        compiler_params=_params(1),
        name="attn_out_ffn",
    )(x, o, w_out.astype(BF16), g.reshape(1, -1), wg.astype(BF16), wu.astype(BF16), wd.astype(BF16))


def _gate_act(g, is_input_gate):
    g = GATE_CAP * jnp.tanh(g / GATE_CAP)
    log_f = jnp.minimum(g, 0.0) - jnp.log1p(jnp.exp(-jnp.abs(g)))
    return jnp.where(is_input_gate, g, log_f)


def _mlstm_proj_kernel(x_ref, g_ref, wqk_ref, wv_ref, wo_ref, wgc_ref, wgt_ref, cw_ref, cb_ref, gbc_ref, gbr_ref,
                       q_ref, k_ref, v_ref, op_ref, gc_ref, gt_ref, ext_sc):
    tm = x_ref.shape[0]

    @pl.when(pl.program_id(0) == 0)
    def _():
        ext_sc[0:HALO, :] = jnp.zeros((HALO, ext_sc.shape[1]), F32)

    xn = _rms(x_ref[...], g_ref[...]).astype(BF16)
    ext_sc[HALO:HALO + tm, :] = _dot(xn, wqk_ref[...])
    conv = cb_ref[...] + cw_ref[CONV_W - 1:CONV_W, :] * ext_sc[HALO:HALO + tm, :]
    for back in range(1, CONV_W):
        w_row = cw_ref[CONV_W - 1 - back:CONV_W - back, :]
        conv = conv + w_row * ext_sc[HALO - back:HALO - back + tm, :]
    ext_sc[0:HALO, :] = ext_sc[tm:tm + HALO, :]
    qk = _silu(conv)
    q_ref[...] = (qk[:, :MLSTM_QK_TOT] * (MLSTM_QK ** -0.5)).astype(BF16)
    k_ref[...] = qk[:, MLSTM_QK_TOT:].astype(BF16)
    v_t = _dot_nt(wv_ref[...], xn)
    ones_rows = (lax.broadcasted_iota(I32, (MLSTM_V_EXT - MLSTM_V, tm), 0) == 0).astype(BF16)
    for h in range(MLSTM_HEADS):
        v_ref[h, 0:MLSTM_V, :] = v_t[h * MLSTM_V:(h + 1) * MLSTM_V, :].astype(BF16)
        v_ref[h, MLSTM_V:MLSTM_V_EXT, :] = ones_rows
    op_ref[...] = _dot(xn, wo_ref[...])
    gc = _dot(xn, wgc_ref[...]) + gbc_ref[...]
    gc_ref[...] = _gate_act(gc, lax.broadcasted_iota(I32, gc.shape, 1) < MLSTM_HEADS)
    gt = _dot_nt(wgt_ref[...], xn) + gbr_ref[...]
    gt_ref[...] = _gate_act(gt, lax.broadcasted_iota(I32, gt.shape, 0) < MLSTM_HEADS)


def _mlstm_proj(x, g, w_in, conv_w, conv_b, gate_b):
    s = x.shape[0]
    tm = TOKEN_TILE
    o1 = 2 * MLSTM_QK_TOT
    o2 = o1 + MLSTM_HEADS * MLSTM_V
    o3 = o2 + D_MODEL
    ng = 2 * MLSTM_HEADS
    w = w_in.astype(BF16)
    return pl.pallas_call(
        _mlstm_proj_kernel,
        grid=(s // tm,),
        in_specs=[
            pl.BlockSpec((tm, D_MODEL), lambda i: (i, 0)),
            _const_spec((1, D_MODEL)),
            _const_spec((D_MODEL, o1)), _const_spec((D_MODEL, o2 - o1)), _const_spec((D_MODEL, o3 - o2)),
            _const_spec((D_MODEL, ng)), _const_spec((ng, D_MODEL)),
            _const_spec((CONV_W, o1)), _const_spec((1, o1)),
            _const_spec((1, ng)), _const_spec((ng, 1)),
        ],
        out_specs=[
            pl.BlockSpec((tm, MLSTM_QK_TOT), lambda i: (i, 0)),
            pl.BlockSpec((tm, MLSTM_QK_TOT), lambda i: (i, 0)),
            pl.BlockSpec((MLSTM_HEADS, MLSTM_V_EXT, tm), lambda i: (0, 0, i)),
            pl.BlockSpec((tm, D_MODEL), lambda i: (i, 0)),
            pl.BlockSpec((tm, ng), lambda i: (i, 0)),
            pl.BlockSpec((ng, tm), lambda i: (0, i)),
        ],
        out_shape=[
            jax.ShapeDtypeStruct((s, MLSTM_QK_TOT), BF16),
            jax.ShapeDtypeStruct((s, MLSTM_QK_TOT), BF16),
            jax.ShapeDtypeStruct((MLSTM_HEADS, MLSTM_V_EXT, s), BF16),
            jax.ShapeDtypeStruct((s, D_MODEL), F32),
            jax.ShapeDtypeStruct((s, ng), F32),
            jax.ShapeDtypeStruct((ng, s), F32),
        ],
        scratch_shapes=[pltpu.VMEM((tm + HALO, o1), F32)],
        compiler_params=_params(1),
        name="mlstm_proj",
    )(x, g.reshape(1, -1), w[:, :o1], w[:, o1:o2].T, w[:, o2:o3], w[:, o3:], w[:, o3:].T, conv_w,
      conv_b.reshape(1, -1), gate_b.reshape(1, -1), gate_b.reshape(-1, 1))


def _mlstm_cell_kernel(q_ref, k_ref, vt_ref, op_ref, gc_ref, gt_ref, hn_ref, tri_ref, y_ref, c_sc, m_sc):
    n = q_ref.shape[0]
    nh = MLSTM_HEADS

    @pl.when(pl.program_id(0) == 0)
    def _():
        c_sc[...] = jnp.zeros(c_sc.shape, F32)
        m_sc[...] = jnp.zeros(m_sc.shape, F32)

    gc = gc_ref[...]
    gt = gt_ref[...]
    ig_c, lf_c = gc[:, :nh], gc[:, nh:]
    ig_r, lf_r = gt[:nh, :], gt[nh:, :]
    tri = tri_ref[...]
    b_c = sum(_dot(tri, part) for part in _split3(lf_c))
    b_r = sum(_dot_nt(part, tri) for part in _split3(lf_r))
    a_c = b_c[n - 1:n, :] - b_c + ig_c
    w_c = jnp.exp(a_c - jnp.max(a_c, axis=0, keepdims=True))
    rc_c = ig_c - b_c
    bl_r = b_r[:, n - 1:n]
    m_prev = m_sc[...]
    m_loc = jnp.max(bl_r - b_r + ig_r, axis=1, keepdims=True)
    m_new = jnp.maximum(bl_r + m_prev, m_loc)
    s_prev = jnp.exp(bl_r + m_prev - m_new)
    s_loc = jnp.exp(m_loc - m_new)
    inter_r = b_r + m_prev
    causal = lax.broadcasted_iota(I32, (n, n), 0) <= lax.broadcasted_iota(I32, (n, n), 1)
    for h in range(nh):
        qh = q_ref[:, h * MLSTM_QK:(h + 1) * MLSTM_QK]
        kh = k_ref[:, h * MLSTM_QK:(h + 1) * MLSTM_QK]
        vt = vt_ref[h]
        d = jnp.where(causal, rc_c[:, h:h + 1] + b_r[h:h + 1, :], NEG)
        inter = inter_r[h:h + 1, :]
        m_t = jnp.maximum(inter, jnp.max(d, axis=0, keepdims=True))
        wts = (jnp.exp(d - m_t) * _dot_nt(kh, qh)).astype(BF16)
        c_prev = c_sc[h]
        num = _dot(vt, wts) + jnp.exp(inter - m_t) * _dot_nt(c_prev.astype(BF16), qh)
        den = num[MLSTM_V:MLSTM_V + 1, :]
        h_t = num[0:MLSTM_V, :] / jnp.maximum(jnp.abs(den), jnp.exp(-m_t))
        h_t = h_t * lax.rsqrt(jnp.mean(h_t * h_t, axis=0, keepdims=True) + EPS)
        sl = slice(h * MLSTM_V, (h + 1) * MLSTM_V)
        y = jax.nn.sigmoid(op_ref[:, sl]) * (h_t.T * hn_ref[:, sl])
        y_ref[:, sl] = y.astype(BF16)
        kw = (kh.astype(F32) * w_c[:, h:h + 1]).astype(BF16)
        c_sc[h] = s_prev[h:h + 1, :] * c_prev + s_loc[h:h + 1, :] * _dot(vt, kw)
    m_sc[...] = m_new


def _mlstm_cell(q, k, v_t, o_pre, gc, gt, head_norm):
    s = q.shape[0]
    n = MLSTM_CHUNK
    ng = 2 * MLSTM_HEADS
    tri = jnp.tril(jnp.ones((n, n), BF16))
    row = lambda w: pl.BlockSpec((n, w), lambda c: (c, 0))
    return pl.pallas_call(
        _mlstm_cell_kernel,
        grid=(s // n,),
        in_specs=[row(MLSTM_QK_TOT), row(MLSTM_QK_TOT),
                  pl.BlockSpec((MLSTM_HEADS, MLSTM_V_EXT, n), lambda c: (0, 0, c)), row(D_MODEL), row(ng),
                  pl.BlockSpec((ng, n), lambda c: (0, c)), _const_spec((1, D_MODEL)), _const_spec((n, n))],
        out_specs=row(D_MODEL),
        out_shape=jax.ShapeDtypeStruct((s, D_MODEL), BF16),
        scratch_shapes=[pltpu.VMEM((MLSTM_HEADS, MLSTM_V_EXT, MLSTM_QK), F32), pltpu.VMEM((MLSTM_HEADS, 1), F32)],
        compiler_params=_params(1),
        name="mlstm_cell",
    )(q, k, v_t, o_pre, gc, gt, head_norm.reshape(1, -1), tri)


def _mlstm_out_router_kernel(x_ref, y_ref, wo_ref, g_ref, rt_ref, x3_ref, hn_ref, idx_ref, gate_ref):
    x3 = x_ref[...] + _dot(y_ref[...], wo_ref[...])
    hn = _rms(x3, g_ref[...])
    x3_ref[...] = x3
    hn_ref[...] = hn
    logits = lax.dot_general(rt_ref[...], hn, (((1,), (1,)), ((), ())), precision=lax.Precision.HIGHEST,
                             preferred_element_type=F32)
    e_iota = lax.broadcasted_iota(I32, logits.shape, 0)
    m1 = jnp.max(logits, axis=0, keepdims=True)
    i1 = jnp.min(jnp.where(logits == m1, e_iota, N_EXPERTS), axis=0, keepdims=True)
    rest = jnp.where(e_iota == i1, -jnp.inf, logits)
    m2 = jnp.max(rest, axis=0, keepdims=True)
    i2 = jnp.min(jnp.where(rest == m2, e_iota, N_EXPERTS), axis=0, keepdims=True)
    e2 = jnp.exp(m2 - m1)
    g1 = 1.0 / (1.0 + e2)
    idx_ref[...] = jnp.where(e_iota == 0, i1, jnp.where(e_iota == 1, i2, 0))
    gate_ref[...] = jnp.where(e_iota == 0, g1, jnp.where(e_iota == 1, e2 * g1, 0.0))


def _mlstm_out_router(x, y, w_out, g, router):
    s = x.shape[0]
    tm = TOKEN_TILE
    tok = pl.BlockSpec((tm, D_MODEL), lambda i: (i, 0))
    lane = pl.BlockSpec((N_EXPERTS, tm), lambda i: (0, i))
    return pl.pallas_call(
        _mlstm_out_router_kernel,
        grid=(s // tm,),
        in_specs=[tok, tok, _const_spec((D_MODEL, D_MODEL)), _const_spec((1, D_MODEL)),
                  _const_spec((N_EXPERTS, D_MODEL))],
        out_specs=[tok, tok, lane, lane],
        out_shape=[jax.ShapeDtypeStruct((s, D_MODEL), F32), jax.ShapeDtypeStruct((s, D_MODEL), F32),
                   jax.ShapeDtypeStruct((N_EXPERTS, s), I32), jax.ShapeDtypeStruct((N_EXPERTS, s), F32)],
        compiler_params=_params(1),
        name="mlstm_out_router",
    )(x, y, w_out.astype(BF16), g.reshape(1, -1), router.T)


def _row_gather_copy(src_hbm, dst, sem, src_row, dst_row):
    return pltpu.make_async_copy(src_hbm.at[pl.ds(src_row, 1), :], dst.at[pl.ds(dst_row, 1), :], sem)


def _moe_kernel(te_ref, nu_ref, tok_ref, tok_next_ref, hn_hbm, wg_ref, wu_ref, wd_ref, y_ref, xf_sc, xb_sc, sem):
    del te_ref
    tr = xb_sc.shape[0]
    t = pl.program_id(0)
    c = pl.program_id(1)
    nc = pl.num_programs(1)
    n_used = nu_ref[0]
    slot = lax.rem(t, 2)
    n_groups, rps = xf_sc.shape[1], xf_sc.shape[2]

    def wait_tile(s):
        for grp in range(n_groups):
            pltpu.make_async_copy(hn_hbm.at[pl.ds(0, rps), :], xf_sc.at[s, grp], sem.at[s]).wait()

    @pl.when((c == 0) & (t == 0) & (n_used > 0))
    def _():
        for grp in range(n_groups):
            def body(r, carry, grp=grp):
                _row_gather_copy(hn_hbm, xf_sc.at[0, grp], sem.at[0], tok_ref[0, 0, grp * rps + r], r).start()
                return carry
            lax.fori_loop(0, rps, body, 0, unroll=8)

    @pl.when((c == 0) & (t < n_used))
    def _():
        wait_tile(slot)
        for grp in range(n_groups):
            xb_sc[grp * rps:(grp + 1) * rps, :] = xf_sc[slot, grp].astype(BF16)

    @pl.when(c == 0)
    def _():
        y_ref[...] = jnp.zeros(y_ref.shape, F32)

    @pl.when(t < n_used)
    def _():
        xb = xb_sc[...]
        a = (_silu(_dot(xb, wg_ref[0])) * _dot(xb, wu_ref[0])).astype(BF16)
        dst = xf_sc.at[1 - slot, c]
        for r in range(rps):
            _row_gather_copy(hn_hbm, dst, sem.at[1 - slot], tok_next_ref[0, 0, c * rps + r], r).start()
        y_ref[...] += _dot(a, wd_ref[0])

    @pl.when((c == nc - 1) & (t == n_used - 1))
    def _():
        wait_tile(1 - slot)


def _moe_experts(hn, tok_sorted, tile_expert, n_used, wg, wu, wd):
    r = tok_sorted.shape[0]
    tr = MOE_ROW_TILE
    nt = r // tr
    nc = D_FF // FF_CHUNK
    tok3 = tok_sorted.reshape(nt, 1, tr)

    def chunk(t, c, nu):
        return jnp.where(t < nu[0], c, nc - 1)

    grid_spec = pltpu.PrefetchScalarGridSpec(
        num_scalar_prefetch=2,
        grid=(nt, nc),
        in_specs=[
            pl.BlockSpec((1, 1, tr), lambda t, c, te, nu: (t, 0, 0), memory_space=pltpu.SMEM),
            pl.BlockSpec((1, 1, tr), lambda t, c, te, nu: (jnp.minimum(t + 1, nt - 1), 0, 0),
                         memory_space=pltpu.SMEM),
            pl.BlockSpec(memory_space=pl.ANY),
            pl.BlockSpec((1, D_MODEL, FF_CHUNK), lambda t, c, te, nu: (te[t], 0, chunk(t, c, nu))),
            pl.BlockSpec((1, D_MODEL, FF_CHUNK), lambda t, c, te, nu: (te[t], 0, chunk(t, c, nu))),
            pl.BlockSpec((1, FF_CHUNK, D_MODEL), lambda t, c, te, nu: (te[t], chunk(t, c, nu), 0)),
        ],
        out_specs=pl.BlockSpec((tr, D_MODEL), lambda t, c, te, nu: (t, 0)),
        scratch_shapes=[pltpu.VMEM((2, nc, tr // nc, D_MODEL), F32), pltpu.VMEM((tr, D_MODEL), BF16),
                        pltpu.SemaphoreType.DMA((2,))],
    )
    return pl.pallas_call(
        _moe_kernel,
        grid_spec=grid_spec,
        out_shape=jax.ShapeDtypeStruct((r, D_MODEL), F32),
        compiler_params=_params(2),
        name="moe_experts",
    )(tile_expert, n_used, tok3, tok3, hn, wg.astype(BF16), wu.astype(BF16), wd.astype(BF16))


def _combine_kernel(r1_ref, r2_ref, r1n_ref, r2n_ref, x_ref, gate_ref, g_ref, y_hbm, out_ref, buf, sem):
    tm = x_ref.shape[0]
    i = pl.program_id(0)
    n = pl.num_programs(0)

    def issue(ra_ref, rb_ref, slot):
        def body(r, carry):
            _row_gather_copy(y_hbm, buf.at[slot, 0], sem.at[slot], ra_ref[0, 0, r], r).start()
            _row_gather_copy(y_hbm, buf.at[slot, 1], sem.at[slot], rb_ref[0, 0, r], r).start()
            return carry
        lax.fori_loop(0, tm, body, 0, unroll=8)

    @pl.when(i == 0)
    def _():
        issue(r1_ref, r2_ref, 0)

    slot = lax.rem(i, 2)

    @pl.when(i + 1 < n)
    def _():
        issue(r1n_ref, r2n_ref, 1 - slot)

    pltpu.make_async_copy(y_hbm.at[pl.ds(0, tm), :], buf.at[slot, 0], sem.at[slot]).wait()
    pltpu.make_async_copy(y_hbm.at[pl.ds(0, tm), :], buf.at[slot, 1], sem.at[slot]).wait()
    gate = gate_ref[...]
    y = x_ref[...] + gate[:, 0:1] * buf[slot, 0] + gate[:, 1:2] * buf[slot, 1]
    out_ref[...] = _rms(y, g_ref[...])


def _moe_combine(x3, y_sorted, rows, gates, g):
    s = x3.shape[0]
    tm = TOKEN_TILE
    nt = s // tm
    r1 = rows[0].reshape(nt, 1, tm)
    r2 = rows[1].reshape(nt, 1, tm)
    cur = pl.BlockSpec((1, 1, tm), lambda i: (i, 0, 0), memory_space=pltpu.SMEM)
    nxt = pl.BlockSpec((1, 1, tm), lambda i: (jnp.minimum(i + 1, nt - 1), 0, 0), memory_space=pltpu.SMEM)
    return pl.pallas_call(
        _combine_kernel,
        grid=(nt,),
        in_specs=[cur, cur, nxt, nxt,
                  pl.BlockSpec((tm, D_MODEL), lambda i: (i, 0)),
                  pl.BlockSpec((tm, TOP_K), lambda i: (i, 0)),
                  _const_spec((1, D_MODEL)),
                  pl.BlockSpec(memory_space=pl.ANY)],
        out_specs=pl.BlockSpec((tm, D_MODEL), lambda i: (i, 0)),
        out_shape=jax.ShapeDtypeStruct((s, D_MODEL), F32),
        scratch_shapes=[pltpu.VMEM((2, TOP_K, tm, D_MODEL), F32), pltpu.SemaphoreType.DMA((2,))],
        compiler_params=_params(1),
        name="moe_combine",
    )(r1, r2, r1, r2, x3, gates, g.reshape(1, -1), y_sorted)


def _route(idx, s):
    tr = MOE_ROW_TILE
    n_rows = TOP_K * s + N_EXPERTS * tr
    n_tiles = n_rows // tr
    e_flat = idx[:TOP_K].reshape(-1)
    onehot = (e_flat[:, None] == jnp.arange(N_EXPERTS, dtype=I32)[None, :]).astype(I32)
    rank = jnp.sum((jnp.cumsum(onehot, axis=0) - onehot) * onehot, axis=1)
    counts = jnp.sum(onehot, axis=0)
    tiles_per = (counts + tr - 1) // tr
    tile_end = jnp.cumsum(tiles_per)
    row_start = (tile_end - tiles_per) * tr
    dest = row_start[e_flat] + rank
    tok = jnp.tile(jnp.arange(s, dtype=I32), TOP_K)
    tok_sorted = jnp.zeros((n_rows,), I32).at[dest].set(tok, unique_indices=True)
    n_used = tile_end[-1]
    tile_ids = jnp.arange(n_tiles, dtype=I32)
    tile_expert = jnp.sum((tile_ids[:, None] >= tile_end[None, :]).astype(I32), axis=1)
    last_expert = jnp.sum((n_used - 1 >= tile_end).astype(I32))
    tile_expert = jnp.where(tile_ids < n_used, tile_expert, last_expert).astype(I32)
    return tok_sorted, tile_expert, n_used.reshape(1).astype(I32), dest.reshape(TOP_K, s).astype(I32)


def kernel(x, positions, norm_mix, norm_ffn, final_norm, mla_w_in, mla_q_norm, mla_w_qb, mla_kv_norm, mla_w_kvb,
           mla_w_out, mlstm_w_in, mlstm_conv_w, mlstm_conv_b, mlstm_gate_b, mlstm_head_norm, mlstm_w_out,
           ffn_w_gate, ffn_w_up, ffn_w_down, moe_router, moe_w_gate, moe_w_up, moe_w_down):
    b, s, d = x.shape
    assert b == 1 and d == D_MODEL and s % TOKEN_TILE == 0 and s % (ATTN_GROUP * ATTN_TILE) == 0
    assert norm_mix.shape[0] == 2, "one attention layer followed by one mLSTM layer"
    x0 = x[0]
    pos = positions[0]
    q_t, k, v_t = _mla_proj(x0, pos, norm_mix[0], mla_w_in[0], mla_q_norm[0], mla_w_qb[0], mla_kv_norm[0],
                            mla_w_kvb[0])
    o = _mla_attn(q_t, k, v_t)
    x2 = _attn_out_ffn(x0, o, mla_w_out[0], norm_ffn[0], ffn_w_gate[0], ffn_w_up[0], ffn_w_down[0])
    q, kk, v, o_pre, gc, gt = _mlstm_proj(x2, norm_mix[1], mlstm_w_in[0], mlstm_conv_w[0], mlstm_conv_b[0],
                                          mlstm_gate_b[0])
    y = _mlstm_cell(q, kk, v, o_pre, gc, gt, mlstm_head_norm[0])
    x3, hn, idx, gates = _mlstm_out_router(x2, y, mlstm_w_out[0], norm_ffn[1], moe_router[0])
    tok_sorted, tile_expert, n_used, rows = _route(idx, s)
    y_sorted = _moe_experts(hn, tok_sorted, tile_expert, n_used, moe_w_gate[0], moe_w_up[0], moe_w_down[0])
    out = _moe_combine(x3, y_sorted, rows, gates[:TOP_K].T, final_norm)
    return out[None]
```

```python
import functools

import jax
import jax.numpy as jnp
from jax import lax
from jax.experimental import pallas as pl
from jax.experimental.pallas import tpu as pltpu

F32 = jnp.float32
BF16 = jnp.bfloat16
I32 = jnp.int32

D_MODEL = 1024
EPS = 1e-6
MLA_HEADS = 8
MLA_Q_LORA = 256
MLA_KV_LORA = 128
MLA_NOPE = 128
MLA_ROPE = 64
MLA_V = 128
MLA_QK = MLA_NOPE + MLA_ROPE
MLA_QK_PAD = 256
MLA_V_EXT = MLA_V + 16
ROPE_THETA = 10000.0
MLSTM_HEADS = 8
MLSTM_V = 128
MLSTM_QK = 64
MLSTM_V_EXT = MLSTM_V + 16
MLSTM_QK_TOT = MLSTM_HEADS * MLSTM_QK
CONV_W = 4
GATE_CAP = 15.0
D_FF = 3584
N_EXPERTS = 8
TOP_K = 2

NEG = -1e30
LOG2_E = 1.4426950408889634

TOKEN_TILE = 512
ATTN_TILE = 512
ATTN_GROUP = 4
MLSTM_CHUNK = 256
MOE_ROW_TILE = 512
FF_CHUNK = 1792
HALO = 8
VMEM_LIMIT = 56 * 1024 * 1024


def _dot(a, b):
    return jnp.dot(a, b, preferred_element_type=F32)


def _dot_nt(a, b):
    return lax.dot_general(a, b, (((1,), (1,)), ((), ())), preferred_element_type=F32)


def _dot_tn(a, b):
    return lax.dot_general(a, b, (((0,), (0,)), ((), ())), preferred_element_type=F32)


def _rms(x, g):
    return x * lax.rsqrt(jnp.mean(x * x, axis=-1, keepdims=True) + EPS) * g


def _silu(x):
    return x * jax.nn.sigmoid(x)


def _split3(x):
    hi = x.astype(BF16)
    r1 = x - hi.astype(F32)
    mid = r1.astype(BF16)
    lo = (r1 - mid.astype(F32)).astype(BF16)
    return hi, mid, lo


def _params(n_axes):
    return pltpu.CompilerParams(dimension_semantics=("arbitrary",) * n_axes, vmem_limit_bytes=VMEM_LIMIT)


def _const_spec(shape):
    return pl.BlockSpec(shape, lambda *_: (0,) * len(shape))


def _mla_proj_kernel(x_ref, posc_ref, posr_ref, g_ref, invr_ref, invc_ref, sgnr_ref, sgnc_ref, win_ref, qn_ref,
                     wqt_ref, kvn_ref, wk_ref, wvt_ref, qt_ref, k_ref, vt_ref):
    tm = x_ref.shape[0]
    xn = _rms(x_ref[...], g_ref[...]).astype(BF16)
    proj = _dot(xn, win_ref[...])
    cq = _rms(proj[:, :MLA_Q_LORA], qn_ref[...]).astype(BF16)
    ckv = _rms(proj[:, MLA_Q_LORA:MLA_Q_LORA + MLA_KV_LORA], kvn_ref[...]).astype(BF16)
    ang_r = posc_ref[...].astype(F32) * invr_ref[...]
    cos_r = jnp.cos(ang_r)
    sin_r = jnp.sin(ang_r) * sgnr_ref[...]
    ang_c = invc_ref[...] * posr_ref[...].astype(F32)
    cos_c = jnp.cos(ang_c)
    sin_c = jnp.sin(ang_c) * sgnc_ref[...]
    scale = MLA_QK ** -0.5 * LOG2_E
    n_nope = MLA_HEADS * MLA_NOPE
    n_rope = MLA_HEADS * MLA_ROPE
    q_t = _dot_nt(wqt_ref[...], cq)
    zq = jnp.zeros((MLA_QK_PAD - MLA_QK, tm), BF16)
    for h in range(MLA_HEADS):
        qt_ref[h, 0:MLA_NOPE, :] = (q_t[h * MLA_NOPE:(h + 1) * MLA_NOPE, :] * scale).astype(BF16)
        a = q_t[n_nope + h * MLA_ROPE:n_nope + (h + 1) * MLA_ROPE, :]
        b = q_t[n_nope + n_rope + h * MLA_ROPE:n_nope + n_rope + (h + 1) * MLA_ROPE, :]
        qt_ref[h, MLA_NOPE:MLA_QK, :] = ((a * cos_c + b * sin_c) * scale).astype(BF16)
        qt_ref[h, MLA_QK:MLA_QK_PAD, :] = zq
    k_nope = _dot(ckv, wk_ref[...])
    o_r = MLA_Q_LORA + MLA_KV_LORA
    k_rot = proj[:, o_r:o_r + MLA_ROPE] * cos_r + proj[:, o_r + MLA_ROPE:o_r + 2 * MLA_ROPE] * sin_r
    k_tail = jnp.concatenate([k_rot, jnp.zeros_like(k_rot)], axis=-1).astype(BF16)
    v_t = _dot_nt(wvt_ref[...], ckv)
    ones_rows = (lax.broadcasted_iota(I32, (MLA_V_EXT - MLA_V, tm), 0) == 0).astype(BF16)
    for h in range(MLA_HEADS):
        k_ref[h, :, 0:MLA_NOPE] = k_nope[:, h * MLA_NOPE:(h + 1) * MLA_NOPE].astype(BF16)
        k_ref[h, :, MLA_NOPE:MLA_QK_PAD] = k_tail
        vt_ref[h, 0, 0:MLA_V, :] = v_t[h * MLA_V:(h + 1) * MLA_V, :].astype(BF16)
        vt_ref[h, 0, MLA_V:MLA_V_EXT, :] = ones_rows


def _mla_proj(x, pos, g, w_in, q_norm, w_qb, kv_norm, w_kvb):
    s = x.shape[0]
    tm = ATTN_TILE
    nt = s // tm
    half = MLA_ROPE // 2
    o_r = MLA_Q_LORA + MLA_KV_LORA
    win = jnp.concatenate([w_in, w_in[:, o_r + half:], w_in[:, o_r:o_r + half]], axis=1).astype(BF16)
    wq = w_qb.reshape(MLA_Q_LORA, MLA_HEADS, MLA_QK)
    wq_rope = wq[:, :, MLA_NOPE:]
    wq_swap = jnp.concatenate([wq_rope[..., half:], wq_rope[..., :half]], axis=-1)
    wqt = jnp.concatenate([wq[:, :, :MLA_NOPE].reshape(MLA_Q_LORA, -1), wq_rope.reshape(MLA_Q_LORA, -1),
                           wq_swap.reshape(MLA_Q_LORA, -1)], axis=1).T.astype(BF16)
    wkv = w_kvb.reshape(MLA_KV_LORA, MLA_HEADS, MLA_NOPE + MLA_V)
    wk = wkv[:, :, :MLA_NOPE].reshape(MLA_KV_LORA, -1).astype(BF16)
    wvt = wkv[:, :, MLA_NOPE:].reshape(MLA_KV_LORA, -1).T.astype(BF16)
    inv = 1.0 / (ROPE_THETA ** (jnp.arange(0, MLA_ROPE, 2, dtype=F32) / MLA_ROPE))
    inv2 = jnp.concatenate([inv, inv])
    sgn = jnp.concatenate([-jnp.ones((half,), F32), jnp.ones((half,), F32)])
    n_q = wqt.shape[0]
    return pl.pallas_call(
        _mla_proj_kernel,
        grid=(nt,),
        in_specs=[
            pl.BlockSpec((tm, D_MODEL), lambda i: (i, 0)),
            pl.BlockSpec((tm, 1), lambda i: (i, 0)),
            pl.BlockSpec((1, tm), lambda i: (0, i)),
            _const_spec((1, D_MODEL)),
            _const_spec((1, MLA_ROPE)), _const_spec((MLA_ROPE, 1)),
            _const_spec((1, MLA_ROPE)), _const_spec((MLA_ROPE, 1)),
            _const_spec((D_MODEL, 512)),
            _const_spec((1, MLA_Q_LORA)),
            _const_spec((n_q, MLA_Q_LORA)),
            _const_spec((1, MLA_KV_LORA)),
            _const_spec((MLA_KV_LORA, MLA_HEADS * MLA_NOPE)),
            _const_spec((MLA_HEADS * MLA_V, MLA_KV_LORA)),
        ],
        out_specs=[
            pl.BlockSpec((MLA_HEADS, MLA_QK_PAD, tm), lambda i: (0, 0, i)),
            pl.BlockSpec((MLA_HEADS, tm, MLA_QK_PAD), lambda i: (0, i, 0)),
            pl.BlockSpec((MLA_HEADS, 1, MLA_V_EXT, tm), lambda i: (0, i, 0, 0)),
        ],
        out_shape=[
            jax.ShapeDtypeStruct((MLA_HEADS, MLA_QK_PAD, s), BF16),
            jax.ShapeDtypeStruct((MLA_HEADS, s, MLA_QK_PAD), BF16),
            jax.ShapeDtypeStruct((MLA_HEADS, nt, MLA_V_EXT, tm), BF16),
        ],
        compiler_params=_params(1),
        name="mla_proj",
    )(x, pos.reshape(s, 1), pos.reshape(1, s), g.reshape(1, -1), inv2.reshape(1, -1), inv2.reshape(-1, 1),
      sgn.reshape(1, -1), sgn.reshape(-1, 1), win, q_norm.reshape(1, -1), wqt, kv_norm.reshape(1, -1), wk, wvt)


def _attn_kernel(qt_ref, k_ref, vt_ref, o_ref, m_sc, acc_sc):
    t = ATTN_TILE
    g = pl.program_id(1)
    m_sc[...] = jnp.full(m_sc.shape, NEG, F32)
    acc_sc[...] = jnp.zeros(acc_sc.shape, F32)

    def scores(item):
        a, j, _ = item
        kj = k_ref[0, pl.ds(pl.multiple_of(j * t, t), t), :]
        return _dot(kj, qt_ref[0, :, a * t:(a + 1) * t])

    def update(item, s):
        a, j, diagonal = item
        if diagonal:
            kpos = lax.broadcasted_iota(I32, (t, t), 0)
            qpos = lax.broadcasted_iota(I32, (t, t), 1)
            s = jnp.where(kpos <= qpos, s, NEG)
        m_old = m_sc[a]
        m_new = jnp.maximum(m_old, jnp.max(s, axis=0, keepdims=True))
        p = jnp.exp2((s - m_new).astype(BF16))
        alpha = jnp.exp2(m_old - m_new)
        acc_sc[a] = alpha * acc_sc[a] + _dot(vt_ref[0, j], p)
        m_sc[a] = m_new

    def run(items):
        s_next = scores(items[0])
        for n, item in enumerate(items):
            s = s_next
            if n + 1 < len(items):
                s_next = scores(items[n + 1])
            update(item, s)

    def body(jj, carry):
        run([(a, ATTN_GROUP * jj + dj, False) for dj in range(ATTN_GROUP) for a in range(ATTN_GROUP)])
        return carry

    lax.fori_loop(0, g, body, 0)
    j0 = ATTN_GROUP * g
    run([(a, j0 + d, a == d) for d in range(ATTN_GROUP) for a in range(d, ATTN_GROUP)])
    for a in range(ATTN_GROUP):
        o_t = acc_sc[a, 0:MLA_V, :] * (1.0 / acc_sc[a, MLA_V:MLA_V + 1, :])
        o_ref[a * t:(a + 1) * t, :] = o_t.T.astype(BF16)


def _mla_attn(q_t, k, v_t):
    s = k.shape[1]
    t = ATTN_TILE
    nt = s // t
    tg = ATTN_GROUP * t
    return pl.pallas_call(
        _attn_kernel,
        grid=(MLA_HEADS, s // tg),
        in_specs=[
            pl.BlockSpec((1, MLA_QK_PAD, tg), lambda h, g: (h, 0, g)),
            pl.BlockSpec((1, s, MLA_QK_PAD), lambda h, g: (h, 0, 0)),
            pl.BlockSpec((1, nt, MLA_V_EXT, t), lambda h, g: (h, 0, 0, 0)),
        ],
        out_specs=pl.BlockSpec((tg, MLA_V), lambda h, g: (g, h)),
        out_shape=jax.ShapeDtypeStruct((s, MLA_HEADS * MLA_V), BF16),
        scratch_shapes=[pltpu.VMEM((ATTN_GROUP, 1, t), F32), pltpu.VMEM((ATTN_GROUP, MLA_V_EXT, t), F32)],
        compiler_params=_params(2),
        name="mla_attn",
    )(q_t, k, v_t)


def _attn_out_ffn_kernel(x_ref, o_ref, wo_ref, g_ref, wg_ref, wu_ref, wd_ref, out_ref):
    x1 = x_ref[...] + _dot(o_ref[...], wo_ref[...])
    h = _rms(x1, g_ref[...]).astype(BF16)
    acc = x1
    for c in range(D_FF // FF_CHUNK):
        sl = slice(c * FF_CHUNK, (c + 1) * FF_CHUNK)
        a = (_silu(_dot(h, wg_ref[:, sl])) * _dot(h, wu_ref[:, sl])).astype(BF16)
        acc = acc + _dot(a, wd_ref[sl, :])
    out_ref[...] = acc


def _resident_spec(shape):
    return pl.BlockSpec(shape, lambda *_: (0,) * len(shape), pipeline_mode=pl.Buffered(1))


def _attn_out_ffn(x, o, w_out, g, wg, wu, wd):
    s = x.shape[0]
    tm = TOKEN_TILE
    return pl.pallas_call(
        _attn_out_ffn_kernel,
        grid=(s // tm,),
        in_specs=[
            pl.BlockSpec((tm, D_MODEL), lambda i: (i, 0)),
            pl.BlockSpec((tm, D_MODEL), lambda i: (i, 0)),
            _resident_spec((D_MODEL, D_MODEL)),
            _const_spec((1, D_MODEL)),
            _resident_spec((D_MODEL, D_FF)),
            _resident_spec((D_MODEL, D_FF)),
            _resident_spec((D_FF, D_MODEL)),
        ],
        out_specs=pl.BlockSpec((tm, D_MODEL), lambda i: (i, 0)),
        out_shape=jax.ShapeDtypeStruct((s, D_MODEL), F32),
        compiler_params=_params(1),
        name="attn_out_ffn",
    )(x, o, w_out.astype(BF16), g.reshape(1, -1), wg.astype(BF16), wu.astype(BF16), wd.astype(BF16))


def _gate_act(g, is_input_gate):
    g = GATE_CAP * jnp.tanh(g / GATE_CAP)
    log_f = jnp.minimum(g, 0.0) - jnp.log1p(jnp.exp(-jnp.abs(g)))
    return jnp.where(is_input_gate, g, log_f)


def _mlstm_proj_kernel(x_ref, g_ref, wqk_ref, wv_ref, wo_ref, wgc_ref, wgt_ref, cw_ref, cb_ref, gbc_ref, gbr_ref,
                       q_ref, k_ref, v_ref, op_ref, gc_ref, gt_ref, ext_sc):
    tm = x_ref.shape[0]

    @pl.when(pl.program_id(0) == 0)
    def _():
        ext_sc[0:HALO, :] = jnp.zeros((HALO, ext_sc.shape[1]), F32)

    xn = _rms(x_ref[...], g_ref[...]).astype(BF16)
    ext_sc[HALO:HALO + tm, :] = _dot(xn, wqk_ref[...])
    conv = cb_ref[...] + cw_ref[CONV_W - 1:CONV_W, :] * ext_sc[HALO:HALO + tm, :]
    for back in range(1, CONV_W):
        w_row = cw_ref[CONV_W - 1 - back:CONV_W - back, :]
        conv = conv + w_row * ext_sc[HALO - back:HALO - back + tm, :]
    ext_sc[0:HALO, :] = ext_sc[tm:tm + HALO, :]
    qk = _silu(conv)
    q_ref[...] = (qk[:, :MLSTM_QK_TOT] * (MLSTM_QK ** -0.5)).astype(BF16)
    k_ref[...] = qk[:, MLSTM_QK_TOT:].astype(BF16)
    v_t = _dot_nt(wv_ref[...], xn)
    ones_rows = (lax.broadcasted_iota(I32, (MLSTM_V_EXT - MLSTM_V, tm), 0) == 0).astype(BF16)
    for h in range(MLSTM_HEADS):
        v_ref[h, 0:MLSTM_V, :] = v_t[h * MLSTM_V:(h + 1) * MLSTM_V, :].astype(BF16)
        v_ref[h, MLSTM_V:MLSTM_V_EXT, :] = ones_rows
    op_ref[...] = _dot(xn, wo_ref[...])
    gc = _dot(xn, wgc_ref[...]) + gbc_ref[...]
    gc_ref[...] = _gate_act(gc, lax.broadcasted_iota(I32, gc.shape, 1) < MLSTM_HEADS)
    gt = _dot_nt(wgt_ref[...], xn) + gbr_ref[...]
    gt_ref[...] = _gate_act(gt, lax.broadcasted_iota(I32, gt.shape, 0) < MLSTM_HEADS)


def _mlstm_proj(x, g, w_in, conv_w, conv_b, gate_b):
    s = x.shape[0]
    tm = TOKEN_TILE
    o1 = 2 * MLSTM_QK_TOT
    o2 = o1 + MLSTM_HEADS * MLSTM_V
    o3 = o2 + D_MODEL
    ng = 2 * MLSTM_HEADS
    w = w_in.astype(BF16)
    return pl.pallas_call(
        _mlstm_proj_kernel,
        grid=(s // tm,),
        in_specs=[
            pl.BlockSpec((tm, D_MODEL), lambda i: (i, 0)),
            _const_spec((1, D_MODEL)),
            _const_spec((D_MODEL, o1)), _const_spec((D_MODEL, o2 - o1)), _const_spec((D_MODEL, o3 - o2)),
            _const_spec((D_MODEL, ng)), _const_spec((ng, D_MODEL)),
            _const_spec((CONV_W, o1)), _const_spec((1, o1)),
            _const_spec((1, ng)), _const_spec((ng, 1)),
        ],
        out_specs=[
            pl.BlockSpec((tm, MLSTM_QK_TOT), lambda i: (i, 0)),
            pl.BlockSpec((tm, MLSTM_QK_TOT), lambda i: (i, 0)),
            pl.BlockSpec((MLSTM_HEADS, MLSTM_V_EXT, tm), lambda i: (0, 0, i)),
            pl.BlockSpec((tm, D_MODEL), lambda i: (i, 0)),
            pl.BlockSpec((tm, ng), lambda i: (i, 0)),
            pl.BlockSpec((ng, tm), lambda i: (0, i)),
        ],
        out_shape=[
            jax.ShapeDtypeStruct((s, MLSTM_QK_TOT), BF16),
            jax.ShapeDtypeStruct((s, MLSTM_QK_TOT), BF16),
            jax.ShapeDtypeStruct((MLSTM_HEADS, MLSTM_V_EXT, s), BF16),
            jax.ShapeDtypeStruct((s, D_MODEL), F32),
            jax.ShapeDtypeStruct((s, ng), F32),
            jax.ShapeDtypeStruct((ng, s), F32),
        ],
        scratch_shapes=[pltpu.VMEM((tm + HALO, o1), F32)],
        compiler_params=_params(1),
        name="mlstm_proj",
    )(x, g.reshape(1, -1), w[:, :o1], w[:, o1:o2].T, w[:, o2:o3], w[:, o3:], w[:, o3:].T, conv_w,
      conv_b.reshape(1, -1), gate_b.reshape(1, -1), gate_b.reshape(-1, 1))


def _mlstm_cell_kernel(q_ref, k_ref, vt_ref, op_ref, gc_ref, gt_ref, hn_ref, tri_ref, y_ref, c_sc, m_sc):
    n = q_ref.shape[0]
    nh = MLSTM_HEADS

    @pl.when(pl.program_id(0) == 0)
    def _():
        c_sc[...] = jnp.zeros(c_sc.shape, F32)
        m_sc[...] = jnp.zeros(m_sc.shape, F32)

    gc = gc_ref[...]
    gt = gt_ref[...]
    ig_c, lf_c = gc[:, :nh], gc[:, nh:]
    ig_r, lf_r = gt[:nh, :], gt[nh:, :]
    tri = tri_ref[...]
    b_c = sum(_dot(tri, part) for part in _split3(lf_c))
    b_r = sum(_dot_nt(part, tri) for part in _split3(lf_r))
    a_c = b_c[n - 1:n, :] - b_c + ig_c
    w_c = jnp.exp(a_c - jnp.max(a_c, axis=0, keepdims=True))
    rc_c = ig_c - b_c
    bl_r = b_r[:, n - 1:n]
    m_prev = m_sc[...]
    m_loc = jnp.max(bl_r - b_r + ig_r, axis=1, keepdims=True)
    m_new = jnp.maximum(bl_r + m_prev, m_loc)
    s_prev = jnp.exp(bl_r + m_prev - m_new)
    s_loc = jnp.exp(m_loc - m_new)
    inter_r = b_r + m_prev
    causal = lax.broadcasted_iota(I32, (n, n), 0) <= lax.broadcasted_iota(I32, (n, n), 1)
    for h in range(nh):
        qh = q_ref[:, h * MLSTM_QK:(h + 1) * MLSTM_QK]
        kh = k_ref[:, h * MLSTM_QK:(h + 1) * MLSTM_QK]
        vt = vt_ref[h]
        d = jnp.where(causal, rc_c[:, h:h + 1] + b_r[h:h + 1, :], NEG)
        inter = inter_r[h:h + 1, :]
        m_t = jnp.maximum(inter, jnp.max(d, axis=0, keepdims=True))
        wts = (jnp.exp(d - m_t) * _dot_nt(kh, qh)).astype(BF16)
        c_prev = c_sc[h]
        num = _dot(vt, wts) + jnp.exp(inter - m_t) * _dot_nt(c_prev.astype(BF16), qh)
        den = num[MLSTM_V:MLSTM_V + 1, :]
        h_t = num[0:MLSTM_V, :] / jnp.maximum(jnp.abs(den), jnp.exp(-m_t))
        h_t = h_t * lax.rsqrt(jnp.mean(h_t * h_t, axis=0, keepdims=True) + EPS)
        sl = slice(h * MLSTM_V, (h + 1) * MLSTM_V)
        y = jax.nn.sigmoid(op_ref[:, sl]) * (h_t.T * hn_ref[:, sl])
        y_ref[:, sl] = y.astype(BF16)
        kw = (kh.astype(F32) * w_c[:, h:h + 1]).astype(BF16)
        c_sc[h] = s_prev[h:h + 1, :] * c_prev + s_loc[h:h + 1, :] * _dot(vt, kw)
    m_sc[...] = m_new


def _mlstm_cell(q, k, v_t, o_pre, gc, gt, head_norm):
    s = q.shape[0]
    n = MLSTM_CHUNK
    ng = 2 * MLSTM_HEADS
    tri = jnp.tril(jnp.ones((n, n), BF16))
    row = lambda w: pl.BlockSpec((n, w), lambda c: (c, 0))
    return pl.pallas_call(
        _mlstm_cell_kernel,
        grid=(s // n,),
        in_specs=[row(MLSTM_QK_TOT), row(MLSTM_QK_TOT),
                  pl.BlockSpec((MLSTM_HEADS, MLSTM_V_EXT, n), lambda c: (0, 0, c)), row(D_MODEL), row(ng),
                  pl.BlockSpec((ng, n), lambda c: (0, c)), _const_spec((1, D_MODEL)), _const_spec((n, n))],
        out_specs=row(D_MODEL),
        out_shape=jax.ShapeDtypeStruct((s, D_MODEL), BF16),
        scratch_shapes=[pltpu.VMEM((MLSTM_HEADS, MLSTM_V_EXT, MLSTM_QK), F32), pltpu.VMEM((MLSTM_HEADS, 1), F32)],
        compiler_params=_params(1),
        name="mlstm_cell",
    )(q, k, v_t, o_pre, gc, gt, head_norm.reshape(1, -1), tri)


def _mlstm_out_router_kernel(x_ref, y_ref, wo_ref, g_ref, rt_ref, x3_ref, hn_ref, idx_ref, gate_ref):
    x3 = x_ref[...] + _dot(y_ref[...], wo_ref[...])
    hn = _rms(x3, g_ref[...])
    x3_ref[...] = x3
    hn_ref[...] = hn
    logits = lax.dot_general(rt_ref[...], hn, (((1,), (1,)), ((), ())), precision=lax.Precision.HIGHEST,
                             preferred_element_type=F32)
    e_iota = lax.broadcasted_iota(I32, logits.shape, 0)
    m1 = jnp.max(logits, axis=0, keepdims=True)
    i1 = jnp.min(jnp.where(logits == m1, e_iota, N_EXPERTS), axis=0, keepdims=True)
    rest = jnp.where(e_iota == i1, -jnp.inf, logits)
    m2 = jnp.max(rest, axis=0, keepdims=True)
    i2 = jnp.min(jnp.where(rest == m2, e_iota, N_EXPERTS), axis=0, keepdims=True)
    e2 = jnp.exp(m2 - m1)
    g1 = 1.0 / (1.0 + e2)
    idx_ref[...] = jnp.where(e_iota == 0, i1, jnp.where(e_iota == 1, i2, 0))
    gate_ref[...] = jnp.where(e_iota == 0, g1, jnp.where(e_iota == 1, e2 * g1, 0.0))


def _mlstm_out_router(x, y, w_out, g, router):
    s = x.shape[0]
    tm = TOKEN_TILE
    tok = pl.BlockSpec((tm, D_MODEL), lambda i: (i, 0))
    lane = pl.BlockSpec((N_EXPERTS, tm), lambda i: (0, i))
    return pl.pallas_call(
        _mlstm_out_router_kernel,
        grid=(s // tm,),
        in_specs=[tok, tok, _const_spec((D_MODEL, D_MODEL)), _const_spec((1, D_MODEL)),
                  _const_spec((N_EXPERTS, D_MODEL))],
        out_specs=[tok, tok, lane, lane],
        out_shape=[jax.ShapeDtypeStruct((s, D_MODEL), F32), jax.ShapeDtypeStruct((s, D_MODEL), F32),
                   jax.ShapeDtypeStruct((N_EXPERTS, s), I32), jax.ShapeDtypeStruct((N_EXPERTS, s), F32)],
        compiler_params=_params(1),
        name="mlstm_out_router",
    )(x, y, w_out.astype(BF16), g.reshape(1, -1), router.T)


def _row_gather_copy(src_hbm, dst, sem, src_row, dst_row):
    return pltpu.make_async_copy(src_hbm.at[pl.ds(src_row, 1), :], dst.at[pl.ds(dst_row, 1), :], sem)


def _moe_kernel(te_ref, nu_ref, tok_ref, tok_next_ref, hn_hbm, wg_ref, wu_ref, wd_ref, y_ref, xf_sc, xb_sc, sem):
    del te_ref
    tr = xb_sc.shape[0]
    t = pl.program_id(0)
    c = pl.program_id(1)
    nc = pl.num_programs(1)
    n_used = nu_ref[0]
    slot = lax.rem(t, 2)
    n_groups, rps = xf_sc.shape[1], xf_sc.shape[2]

    def wait_tile(s):
        for grp in range(n_groups):
            pltpu.make_async_copy(hn_hbm.at[pl.ds(0, rps), :], xf_sc.at[s, grp], sem.at[s]).wait()

    @pl.when((c == 0) & (t == 0) & (n_used > 0))
    def _():
        for grp in range(n_groups):
            def body(r, carry, grp=grp):
                _row_gather_copy(hn_hbm, xf_sc.at[0, grp], sem.at[0], tok_ref[0, 0, grp * rps + r], r).start()
                return carry
            lax.fori_loop(0, rps, body, 0, unroll=8)

    @pl.when((c == 0) & (t < n_used))
    def _():
        wait_tile(slot)
        for grp in range(n_groups):
            xb_sc[grp * rps:(grp + 1) * rps, :] = xf_sc[slot, grp].astype(BF16)

    @pl.when(c == 0)
    def _():
        y_ref[...] = jnp.zeros(y_ref.shape, F32)

    @pl.when(t < n_used)
    def _():
        xb = xb_sc[...]
        a = (_silu(_dot(xb, wg_ref[0])) * _dot(xb, wu_ref[0])).astype(BF16)
        dst = xf_sc.at[1 - slot, c]
        for r in range(rps):
            _row_gather_copy(hn_hbm, dst, sem.at[1 - slot], tok_next_ref[0, 0, c * rps + r], r).start()
        y_ref[...] += _dot(a, wd_ref[0])

    @pl.when((c == nc - 1) & (t == n_used - 1))
    def _():
        wait_tile(1 - slot)


def _moe_experts(hn, tok_sorted, tile_expert, n_used, wg, wu, wd):
    r = tok_sorted.shape[0]
    tr = MOE_ROW_TILE
    nt = r // tr
    nc = D_FF // FF_CHUNK
    tok3 = tok_sorted.reshape(nt, 1, tr)

    def chunk(t, c, nu):
        return jnp.where(t < nu[0], c, nc - 1)

    grid_spec = pltpu.PrefetchScalarGridSpec(
        num_scalar_prefetch=2,
        grid=(nt, nc),
        in_specs=[
            pl.BlockSpec((1, 1, tr), lambda t, c, te, nu: (t, 0, 0), memory_space=pltpu.SMEM),
            pl.BlockSpec((1, 1, tr), lambda t, c, te, nu: (jnp.minimum(t + 1, nt - 1), 0, 0),
                         memory_space=pltpu.SMEM),
            pl.BlockSpec(memory_space=pl.ANY),
            pl.BlockSpec((1, D_MODEL, FF_CHUNK), lambda t, c, te, nu: (te[t], 0, chunk(t, c, nu))),
            pl.BlockSpec((1, D_MODEL, FF_CHUNK), lambda t, c, te, nu: (te[t], 0, chunk(t, c, nu))),
            pl.BlockSpec((1, FF_CHUNK, D_MODEL), lambda t, c, te, nu: (te[t], chunk(t, c, nu), 0)),
        ],
        out_specs=pl.BlockSpec((tr, D_MODEL), lambda t, c, te, nu: (t, 0)),
        scratch_shapes=[pltpu.VMEM((2, nc, tr // nc, D_MODEL), F32), pltpu.VMEM((tr, D_MODEL), BF16),
                        pltpu.SemaphoreType.DMA((2,))],
    )
    return pl.pallas_call(
        _moe_kernel,
        grid_spec=grid_spec,
        out_shape=jax.ShapeDtypeStruct((r, D_MODEL), F32),
        compiler_params=_params(2),
        name="moe_experts",
    )(tile_expert, n_used, tok3, tok3, hn, wg.astype(BF16), wu.astype(BF16), wd.astype(BF16))


def _combine_kernel(r1_ref, r2_ref, x_ref, gate_ref, g_ref, y_hbm, out_ref, buf, sem):
    tm = x_ref.shape[0]
    i = pl.program_id(0)
    n_tiles = pl.num_programs(0) - 1

    @pl.when(i < n_tiles)
    def _():
        slot = lax.rem(i, 2)
        for r in range(tm):
            _row_gather_copy(y_hbm, buf.at[slot, 0], sem.at[slot], r1_ref[0, 0, r], r).start()
            _row_gather_copy(y_hbm, buf.at[slot, 1], sem.at[slot], r2_ref[0, 0, r], r).start()

    @pl.when(i > 0)
    def _():
        slot = lax.rem(i - 1, 2)
        pltpu.make_async_copy(y_hbm.at[pl.ds(0, tm), :], buf.at[slot, 0], sem.at[slot]).wait()
        pltpu.make_async_copy(y_hbm.at[pl.ds(0, tm), :], buf.at[slot, 1], sem.at[slot]).wait()
        gate = gate_ref[...]
        y = x_ref[...] + gate[:, 0:1] * buf[slot, 0] + gate[:, 1:2] * buf[slot, 1]
        out_ref[...] = _rms(y, g_ref[...])


def _moe_combine(x3, y_sorted, rows, gates, g):
    s = x3.shape[0]
    tm = TOKEN_TILE
    nt = s // tm
    r1 = rows[0].reshape(nt, 1, tm)
    r2 = rows[1].reshape(nt, 1, tm)
    started = pl.BlockSpec((1, 1, tm), lambda i: (jnp.minimum(i, nt - 1), 0, 0), memory_space=pltpu.SMEM)
    finished = lambda w: pl.BlockSpec((tm, w), lambda i: (jnp.maximum(i - 1, 0), 0))
    return pl.pallas_call(
        _combine_kernel,
        grid=(nt + 1,),
        in_specs=[started, started, finished(D_MODEL), finished(TOP_K), _const_spec((1, D_MODEL)),
                  pl.BlockSpec(memory_space=pl.ANY)],
        out_specs=finished(D_MODEL),
        out_shape=jax.ShapeDtypeStruct((s, D_MODEL), F32),
        scratch_shapes=[pltpu.VMEM((2, TOP_K, tm, D_MODEL), F32), pltpu.SemaphoreType.DMA((2,))],
        compiler_params=_params(1),
        name="moe_combine",
    )(r1, r2, x3, gates, g.reshape(1, -1), y_sorted)


def _route(idx, s):
    tr = MOE_ROW_TILE
    n_rows = TOP_K * s + N_EXPERTS * tr
    n_tiles = n_rows // tr
    e_flat = idx[:TOP_K].reshape(-1)
    onehot = (e_flat[:, None] == jnp.arange(N_EXPERTS, dtype=I32)[None, :]).astype(I32)
    rank = jnp.sum((jnp.cumsum(onehot, axis=0) - onehot) * onehot, axis=1)
    counts = jnp.sum(onehot, axis=0)
    tiles_per = (counts + tr - 1) // tr
    tile_end = jnp.cumsum(tiles_per)
    row_start = (tile_end - tiles_per) * tr
    dest = row_start[e_flat] + rank
    tok = jnp.tile(jnp.arange(s, dtype=I32), TOP_K)
    tok_sorted = jnp.zeros((n_rows,), I32).at[dest].set(tok, unique_indices=True)
    n_used = tile_end[-1]
    tile_ids = jnp.arange(n_tiles, dtype=I32)
    tile_expert = jnp.sum((tile_ids[:, None] >= tile_end[None, :]).astype(I32), axis=1)
    last_expert = jnp.sum((n_used - 1 >= tile_end).astype(I32))
    tile_expert = jnp.where(tile_ids < n_used, tile_expert, last_expert).astype(I32)
    return tok_sorted, tile_expert, n_used.reshape(1).astype(I32), dest.reshape(TOP_K, s).astype(I32)


def kernel(x, positions, norm_mix, norm_ffn, final_norm, mla_w_in, mla_q_norm, mla_w_qb, mla_kv_norm, mla_w_kvb,
           mla_w_out, mlstm_w_in, mlstm_conv_w, mlstm_conv_b, mlstm_gate_b, mlstm_head_norm, mlstm_w_out,
           ffn_w_gate, ffn_w_up, ffn_w_down, moe_router, moe_w_gate, moe_w_up, moe_w_down):
    b, s, d = x.shape
    assert b == 1 and d == D_MODEL and s % TOKEN_TILE == 0 and s % (ATTN_GROUP * ATTN_TILE) == 0
    assert norm_mix.shape[0] == 2, "one attention layer followed by one mLSTM layer"
    x0 = x[0]
    pos = positions[0]
    q_t, k, v_t = _mla_proj(x0, pos, norm_mix[0], mla_w_in[0], mla_q_norm[0], mla_w_qb[0], mla_kv_norm[0],
                            mla_w_kvb[0])
    o = _mla_attn(q_t, k, v_t)
    x2 = _attn_out_ffn(x0, o, mla_w_out[0], norm_ffn[0], ffn_w_gate[0], ffn_w_up[0], ffn_w_down[0])
    q, kk, v, o_pre, gc, gt = _mlstm_proj(x2, norm_mix[1], mlstm_w_in[0], mlstm_conv_w[0], mlstm_conv_b[0],
                                          mlstm_gate_b[0])
    y = _mlstm_cell(q, kk, v, o_pre, gc, gt, mlstm_head_norm[0])
    x3, hn, idx, gates = _mlstm_out_router(x2, y, mlstm_w_out[0], norm_ffn[1], moe_router[0])
    tok_sorted, tile_expert, n_used, rows = _route(idx, s)
    y_sorted = _moe_experts(hn, tok_sorted, tile_expert, n_used, moe_w_gate[0], moe_w_up[0], moe_w_down[0])
    out = _moe_combine(x3, y_sorted, rows, gates[:TOP_K].T, final_norm)
    return out[None]
```

```python
import functools

import jax
import jax.numpy as jnp
from jax import lax
from jax.experimental import pallas as pl
from jax.experimental.pallas import tpu as pltpu

F32 = jnp.float32
BF16 = jnp.bfloat16
I32 = jnp.int32

D_MODEL = 1024
EPS = 1e-6
MLA_HEADS = 8
MLA_Q_LORA = 256
MLA_KV_LORA = 128
MLA_NOPE = 128
MLA_ROPE = 64
MLA_V = 128
MLA_QK = MLA_NOPE + MLA_ROPE
MLA_QK_PAD = 256
MLA_V_EXT = MLA_V + 16
ROPE_THETA = 10000.0
MLSTM_HEADS = 8
MLSTM_V = 128
MLSTM_QK = 64
MLSTM_V_EXT = MLSTM_V + 16
MLSTM_QK_TOT = MLSTM_HEADS * MLSTM_QK
CONV_W = 4
GATE_CAP = 15.0
D_FF = 3584
N_EXPERTS = 8
TOP_K = 2

NEG = -1e30
LOG2_E = 1.4426950408889634

TOKEN_TILE = 512
ATTN_TILE = 512
ATTN_GROUP = 4
MLSTM_CHUNK = 256
MOE_ROW_TILE = 512
FF_CHUNK = 1792
HALO = 8
VMEM_LIMIT = 56 * 1024 * 1024


def _dot(a, b):
    return jnp.dot(a, b, preferred_element_type=F32)


def _dot_nt(a, b):
    return lax.dot_general(a, b, (((1,), (1,)), ((), ())), preferred_element_type=F32)


def _dot_tn(a, b):
    return lax.dot_general(a, b, (((0,), (0,)), ((), ())), preferred_element_type=F32)


def _rms(x, g):
    return x * lax.rsqrt(jnp.mean(x * x, axis=-1, keepdims=True) + EPS) * g


def _silu(x):
    return x * jax.nn.sigmoid(x)


def _split3(x):
    hi = x.astype(BF16)
    r1 = x - hi.astype(F32)
    mid = r1.astype(BF16)
    lo = (r1 - mid.astype(F32)).astype(BF16)
    return hi, mid, lo


def _params(n_axes):
    return pltpu.CompilerParams(dimension_semantics=("arbitrary",) * n_axes, vmem_limit_bytes=VMEM_LIMIT)


def _const_spec(shape):
    return pl.BlockSpec(shape, lambda *_: (0,) * len(shape))


def _mla_proj_kernel(x_ref, pos_ref, g_ref, inv_ref, win_ref, qn_ref, wqt_ref, kvn_ref, wk_ref, wvt_ref,
                     qt_ref, k_ref, vt_ref):
    tm = x_ref.shape[0]
    xn = _rms(x_ref[...], g_ref[...]).astype(BF16)
    proj = _dot(xn, win_ref[...])
    cq = _rms(proj[:, :MLA_Q_LORA], qn_ref[...]).astype(BF16)
    ckv = _rms(proj[:, MLA_Q_LORA:MLA_Q_LORA + MLA_KV_LORA], kvn_ref[...]).astype(BF16)
    ang = inv_ref[...] * pos_ref[...].astype(F32)
    cos_h = jnp.cos(ang)
    sin_h = jnp.sin(ang)
    cos_c = jnp.concatenate([cos_h, cos_h], axis=0)
    sin_c = jnp.concatenate([-sin_h, sin_h], axis=0)
    cos_r = cos_c.T
    sin_r = sin_c.T
    scale = MLA_QK ** -0.5 * LOG2_E
    n_nope = MLA_HEADS * MLA_NOPE
    n_rope = MLA_HEADS * MLA_ROPE
    q_t = _dot_nt(wqt_ref[...], cq)
    zq = jnp.zeros((MLA_QK_PAD - MLA_QK, tm), BF16)
    for h in range(MLA_HEADS):
        qt_ref[h, 0:MLA_NOPE, :] = (q_t[h * MLA_NOPE:(h + 1) * MLA_NOPE, :] * scale).astype(BF16)
        a = q_t[n_nope + h * MLA_ROPE:n_nope + (h + 1) * MLA_ROPE, :]
        b = q_t[n_nope + n_rope + h * MLA_ROPE:n_nope + n_rope + (h + 1) * MLA_ROPE, :]
        qt_ref[h, MLA_NOPE:MLA_QK, :] = ((a * cos_c + b * sin_c) * scale).astype(BF16)
        qt_ref[h, MLA_QK:MLA_QK_PAD, :] = zq
    k_nope = _dot(ckv, wk_ref[...])
    o_r = MLA_Q_LORA + MLA_KV_LORA
    k_rot = proj[:, o_r:o_r + MLA_ROPE] * cos_r + proj[:, o_r + MLA_ROPE:o_r + 2 * MLA_ROPE] * sin_r
    k_tail = jnp.concatenate([k_rot, jnp.zeros_like(k_rot)], axis=-1).astype(BF16)
    v_t = _dot_nt(wvt_ref[...], ckv)
    ones_rows = (lax.broadcasted_iota(I32, (MLA_V_EXT - MLA_V, tm), 0) == 0).astype(BF16)
    for h in range(MLA_HEADS):
        k_ref[h, :, 0:MLA_NOPE] = k_nope[:, h * MLA_NOPE:(h + 1) * MLA_NOPE].astype(BF16)
        k_ref[h, :, MLA_NOPE:MLA_QK_PAD] = k_tail
        vt_ref[h, 0, 0:MLA_V, :] = v_t[h * MLA_V:(h + 1) * MLA_V, :].astype(BF16)
        vt_ref[h, 0, MLA_V:MLA_V_EXT, :] = ones_rows


def _mla_proj(x, pos, g, w_in, q_norm, w_qb, kv_norm, w_kvb):
    s = x.shape[0]
    tm = ATTN_TILE
    nt = s // tm
    half = MLA_ROPE // 2
    o_r = MLA_Q_LORA + MLA_KV_LORA
    win = jnp.concatenate([w_in, w_in[:, o_r + half:], w_in[:, o_r:o_r + half]], axis=1).astype(BF16)
    wq = w_qb.reshape(MLA_Q_LORA, MLA_HEADS, MLA_QK)
    wq_rope = wq[:, :, MLA_NOPE:]
    wq_swap = jnp.concatenate([wq_rope[..., half:], wq_rope[..., :half]], axis=-1)
    wqt = jnp.concatenate([wq[:, :, :MLA_NOPE].reshape(MLA_Q_LORA, -1), wq_rope.reshape(MLA_Q_LORA, -1),
                           wq_swap.reshape(MLA_Q_LORA, -1)], axis=1).T.astype(BF16)
    wkv = w_kvb.reshape(MLA_KV_LORA, MLA_HEADS, MLA_NOPE + MLA_V)
    wk = wkv[:, :, :MLA_NOPE].reshape(MLA_KV_LORA, -1).astype(BF16)
    wvt = wkv[:, :, MLA_NOPE:].reshape(MLA_KV_LORA, -1).T.astype(BF16)
    inv = 1.0 / (ROPE_THETA ** (jnp.arange(0, MLA_ROPE, 2, dtype=F32) / MLA_ROPE))
    n_q = wqt.shape[0]
    return pl.pallas_call(
        _mla_proj_kernel,
        grid=(nt,),
        in_specs=[
            pl.BlockSpec((tm, D_MODEL), lambda i: (i, 0)),
            pl.BlockSpec((1, tm), lambda i: (0, i)),
            _const_spec((1, D_MODEL)),
            _const_spec((half, 1)),
            _const_spec((D_MODEL, 512)),
            _const_spec((1, MLA_Q_LORA)),
            _const_spec((n_q, MLA_Q_LORA)),
            _const_spec((1, MLA_KV_LORA)),
            _const_spec((MLA_KV_LORA, MLA_HEADS * MLA_NOPE)),
            _const_spec((MLA_HEADS * MLA_V, MLA_KV_LORA)),
        ],
        out_specs=[
            pl.BlockSpec((MLA_HEADS, MLA_QK_PAD, tm), lambda i: (0, 0, i)),
            pl.BlockSpec((MLA_HEADS, tm, MLA_QK_PAD), lambda i: (0, i, 0)),
            pl.BlockSpec((MLA_HEADS, 1, MLA_V_EXT, tm), lambda i: (0, i, 0, 0)),
        ],
        out_shape=[
            jax.ShapeDtypeStruct((MLA_HEADS, MLA_QK_PAD, s), BF16),
            jax.ShapeDtypeStruct((MLA_HEADS, s, MLA_QK_PAD), BF16),
            jax.ShapeDtypeStruct((MLA_HEADS, nt, MLA_V_EXT, tm), BF16),
        ],
        compiler_params=_params(1),
        name="mla_proj",
    )(x, pos.reshape(1, s), g.reshape(1, -1), inv.reshape(-1, 1), win, q_norm.reshape(1, -1), wqt,
      kv_norm.reshape(1, -1), wk, wvt)


def _attn_kernel(qt_ref, k_ref, vt_ref, o_ref, m_sc, acc_sc, s_sc):
    t = ATTN_TILE
    g = pl.program_id(1)
    m_sc[...] = jnp.full(m_sc.shape, NEG, F32)
    acc_sc[...] = jnp.zeros(acc_sc.shape, F32)

    def scores(item):
        a, j, _ = item
        kj = k_ref[0, pl.ds(pl.multiple_of(j * t, t), t), :]
        return _dot(kj, qt_ref[0, :, a * t:(a + 1) * t])

    def update(item, s):
        a, j, diagonal = item
        if diagonal:
            kpos = lax.broadcasted_iota(I32, (t, t), 0)
            qpos = lax.broadcasted_iota(I32, (t, t), 1)
            s = jnp.where(kpos <= qpos, s, NEG)
        m_old = m_sc[a]
        m_new = jnp.maximum(m_old, jnp.max(s, axis=0, keepdims=True))
        p = jnp.exp2((s - m_new).astype(BF16))
        alpha = jnp.exp2(m_old - m_new)
        acc_sc[a] = alpha * acc_sc[a] + _dot(vt_ref[0, j], p)
        m_sc[a] = m_new

    def run(items, following=None):
        s_next = s_sc[...]
        for n, item in enumerate(items):
            s = s_next
            if n + 1 < len(items):
                s_next = scores(items[n + 1])
            elif following is not None:
                s_next = scores(following)
            update(item, s)
        if following is not None:
            s_sc[...] = s_next

    def body(jj, carry):
        run([(a, ATTN_GROUP * jj + dj, False) for dj in range(ATTN_GROUP) for a in range(ATTN_GROUP)],
            following=(0, ATTN_GROUP * (jj + 1), False))
        return carry

    s_sc[...] = scores((0, 0, False))
    lax.fori_loop(0, g, body, 0)
    j0 = ATTN_GROUP * g
    run([(a, j0 + d, a == d) for d in range(ATTN_GROUP) for a in range(d, ATTN_GROUP)])
    for a in range(ATTN_GROUP):
        o_t = acc_sc[a, 0:MLA_V, :] * (1.0 / acc_sc[a, MLA_V:MLA_V + 1, :])
        o_ref[a * t:(a + 1) * t, :] = o_t.T.astype(BF16)


def _mla_attn(q_t, k, v_t):
    s = k.shape[1]
    t = ATTN_TILE
    nt = s // t
    tg = ATTN_GROUP * t
    return pl.pallas_call(
        _attn_kernel,
        grid=(MLA_HEADS, s // tg),
        in_specs=[
            pl.BlockSpec((1, MLA_QK_PAD, tg), lambda h, g: (h, 0, g)),
            pl.BlockSpec((1, s, MLA_QK_PAD), lambda h, g: (h, 0, 0)),
            pl.BlockSpec((1, nt, MLA_V_EXT, t), lambda h, g: (h, 0, 0, 0)),
        ],
        out_specs=pl.BlockSpec((tg, MLA_V), lambda h, g: (g, h)),
        out_shape=jax.ShapeDtypeStruct((s, MLA_HEADS * MLA_V), BF16),
        scratch_shapes=[pltpu.VMEM((ATTN_GROUP, 1, t), F32), pltpu.VMEM((ATTN_GROUP, MLA_V_EXT, t), F32),
                        pltpu.VMEM((t, t), F32)],
        compiler_params=_params(2),
        name="mla_attn",
    )(q_t, k, v_t)


def _attn_out_ffn_kernel(x_ref, o_ref, wo_ref, g_ref, wg_ref, wu_ref, wd_ref, out_ref):
    x1 = x_ref[...] + _dot(o_ref[...], wo_ref[...])
    h = _rms(x1, g_ref[...]).astype(BF16)
    acc = x1
    for c in range(D_FF // FF_CHUNK):
        sl = slice(c * FF_CHUNK, (c + 1) * FF_CHUNK)
        a = (_silu(_dot(h, wg_ref[:, sl])) * _dot(h, wu_ref[:, sl])).astype(BF16)
        acc = acc + _dot(a, wd_ref[sl, :])
    out_ref[...] = acc


def _resident_spec(shape):
    return pl.BlockSpec(shape, lambda *_: (0,) * len(shape), pipeline_mode=pl.Buffered(1))


def _attn_out_ffn(x, o, w_out, g, wg, wu, wd):
    s = x.shape[0]
    tm = TOKEN_TILE
    return pl.pallas_call(
        _attn_out_ffn_kernel,
        grid=(s // tm,),
        in_specs=[
            pl.BlockSpec((tm, D_MODEL), lambda i: (i, 0)),
            pl.BlockSpec((tm, D_MODEL), lambda i: (i, 0)),
            _resident_spec((D_MODEL, D_MODEL)),
            _const_spec((1, D_MODEL)),
            _resident_spec((D_MODEL, D_FF)),
            _resident_spec((D_MODEL, D_FF)),
            _resident_spec((D_FF, D_MODEL)),
        ],
        out_specs=pl.BlockSpec((tm, D_MODEL), lambda i: (i, 0)),
        out_shape=jax.ShapeDtypeStruct((s, D_MODEL), F32),
        compiler_params=_params(1),
        name="attn_out_ffn",
    )(x, o, w_out.astype(BF16), g.reshape(1, -1), wg.astype(BF16), wu.astype(BF16), wd.astype(BF16))


def _gate_act(g, is_input_gate):
    g = GATE_CAP * jnp.tanh(g / GATE_CAP)
    log_f = jnp.minimum(g, 0.0) - jnp.log1p(jnp.exp(-jnp.abs(g)))
    return jnp.where(is_input_gate, g, log_f)


def _mlstm_proj_kernel(x_ref, g_ref, wqk_ref, wv_ref, wo_ref, wgc_ref, wgt_ref, cw_ref, cb_ref, gbc_ref, gbr_ref,
                       q_ref, k_ref, v_ref, op_ref, gc_ref, gt_ref, ext_sc):
    tm = x_ref.shape[0]

    @pl.when(pl.program_id(0) == 0)
    def _():
        ext_sc[0:HALO, :] = jnp.zeros((HALO, ext_sc.shape[1]), F32)

    xn = _rms(x_ref[...], g_ref[...]).astype(BF16)
    ext_sc[HALO:HALO + tm, :] = _dot(xn, wqk_ref[...])
    conv = cb_ref[...] + cw_ref[CONV_W - 1:CONV_W, :] * ext_sc[HALO:HALO + tm, :]
    for back in range(1, CONV_W):
        w_row = cw_ref[CONV_W - 1 - back:CONV_W - back, :]
        conv = conv + w_row * ext_sc[HALO - back:HALO - back + tm, :]
    ext_sc[0:HALO, :] = ext_sc[tm:tm + HALO, :]
    qk = _silu(conv)
    q_ref[...] = (qk[:, :MLSTM_QK_TOT] * (MLSTM_QK ** -0.5)).astype(BF16)
    k_ref[...] = qk[:, MLSTM_QK_TOT:].astype(BF16)
    v_t = _dot_nt(wv_ref[...], xn)
    ones_rows = (lax.broadcasted_iota(I32, (MLSTM_V_EXT - MLSTM_V, tm), 0) == 0).astype(BF16)
    for h in range(MLSTM_HEADS):
        v_ref[h, 0:MLSTM_V, :] = v_t[h * MLSTM_V:(h + 1) * MLSTM_V, :].astype(BF16)
        v_ref[h, MLSTM_V:MLSTM_V_EXT, :] = ones_rows
    op_ref[...] = _dot(xn, wo_ref[...])
    gc = _dot(xn, wgc_ref[...]) + gbc_ref[...]
    gc_ref[...] = _gate_act(gc, lax.broadcasted_iota(I32, gc.shape, 1) < MLSTM_HEADS)
    gt = _dot_nt(wgt_ref[...], xn) + gbr_ref[...]
    gt_ref[...] = _gate_act(gt, lax.broadcasted_iota(I32, gt.shape, 0) < MLSTM_HEADS)


def _mlstm_proj(x, g, w_in, conv_w, conv_b, gate_b):
    s = x.shape[0]
    tm = TOKEN_TILE
    o1 = 2 * MLSTM_QK_TOT
    o2 = o1 + MLSTM_HEADS * MLSTM_V
    o3 = o2 + D_MODEL
    ng = 2 * MLSTM_HEADS
    w = w_in.astype(BF16)
    return pl.pallas_call(
        _mlstm_proj_kernel,
        grid=(s // tm,),
        in_specs=[
            pl.BlockSpec((tm, D_MODEL), lambda i: (i, 0)),
            _const_spec((1, D_MODEL)),
            _const_spec((D_MODEL, o1)), _const_spec((D_MODEL, o2 - o1)), _const_spec((D_MODEL, o3 - o2)),
            _const_spec((D_MODEL, ng)), _const_spec((ng, D_MODEL)),
            _const_spec((CONV_W, o1)), _const_spec((1, o1)),
            _const_spec((1, ng)), _const_spec((ng, 1)),
        ],
        out_specs=[
            pl.BlockSpec((tm, MLSTM_QK_TOT), lambda i: (i, 0)),
            pl.BlockSpec((tm, MLSTM_QK_TOT), lambda i: (i, 0)),
            pl.BlockSpec((MLSTM_HEADS, MLSTM_V_EXT, tm), lambda i: (0, 0, i)),
            pl.BlockSpec((tm, D_MODEL), lambda i: (i, 0)),
            pl.BlockSpec((tm, ng), lambda i: (i, 0)),
            pl.BlockSpec((ng, tm), lambda i: (0, i)),
        ],
        out_shape=[
            jax.ShapeDtypeStruct((s, MLSTM_QK_TOT), BF16),
            jax.ShapeDtypeStruct((s, MLSTM_QK_TOT), BF16),
            jax.ShapeDtypeStruct((MLSTM_HEADS, MLSTM_V_EXT, s), BF16),
            jax.ShapeDtypeStruct((s, D_MODEL), F32),
            jax.ShapeDtypeStruct((s, ng), F32),
            jax.ShapeDtypeStruct((ng, s), F32),
        ],
        scratch_shapes=[pltpu.VMEM((tm + HALO, o1), F32)],
        compiler_params=_params(1),
        name="mlstm_proj",
    )(x, g.reshape(1, -1), w[:, :o1], w[:, o1:o2].T, w[:, o2:o3], w[:, o3:], w[:, o3:].T, conv_w,
      conv_b.reshape(1, -1), gate_b.reshape(1, -1), gate_b.reshape(-1, 1))


def _mlstm_cell_kernel(q_ref, k_ref, vt_ref, op_ref, gc_ref, gt_ref, hn_ref, tri_ref, y_ref, c_sc, m_sc):
    n = q_ref.shape[0]
    nh = MLSTM_HEADS

    @pl.when(pl.program_id(0) == 0)
    def _():
        c_sc[...] = jnp.zeros(c_sc.shape, F32)
        m_sc[...] = jnp.zeros(m_sc.shape, F32)

    gc = gc_ref[...]
    gt = gt_ref[...]
    ig_c, lf_c = gc[:, :nh], gc[:, nh:]
    ig_r, lf_r = gt[:nh, :], gt[nh:, :]
    tri = tri_ref[...]
    b_c = sum(_dot(tri, part) for part in _split3(lf_c))
    b_r = sum(_dot_nt(part, tri) for part in _split3(lf_r))
    a_c = b_c[n - 1:n, :] - b_c + ig_c
    w_c = jnp.exp(a_c - jnp.max(a_c, axis=0, keepdims=True))
    rc_c = ig_c - b_c
    bl_r = b_r[:, n - 1:n]
    m_prev = m_sc[...]
    m_loc = jnp.max(bl_r - b_r + ig_r, axis=1, keepdims=True)
    m_new = jnp.maximum(bl_r + m_prev, m_loc)
    s_prev = jnp.exp(bl_r + m_prev - m_new)
    s_loc = jnp.exp(m_loc - m_new)
    inter_r = b_r + m_prev
    causal = lax.broadcasted_iota(I32, (n, n), 0) <= lax.broadcasted_iota(I32, (n, n), 1)
    for h in range(nh):
        qh = q_ref[:, h * MLSTM_QK:(h + 1) * MLSTM_QK]
        kh = k_ref[:, h * MLSTM_QK:(h + 1) * MLSTM_QK]
        vt = vt_ref[h]
        d = jnp.where(causal, rc_c[:, h:h + 1] + b_r[h:h + 1, :], NEG)
        inter = inter_r[h:h + 1, :]
        m_t = jnp.maximum(inter, jnp.max(d, axis=0, keepdims=True))
        wts = (jnp.exp(d - m_t) * _dot_nt(kh, qh)).astype(BF16)
        c_prev = c_sc[h]
        num = _dot(vt, wts) + jnp.exp(inter - m_t) * _dot_nt(c_prev.astype(BF16), qh)
        den = num[MLSTM_V:MLSTM_V + 1, :]
        h_t = num[0:MLSTM_V, :] / jnp.maximum(jnp.abs(den), jnp.exp(-m_t))
        h_t = h_t * lax.rsqrt(jnp.mean(h_t * h_t, axis=0, keepdims=True) + EPS)
        sl = slice(h * MLSTM_V, (h + 1) * MLSTM_V)
        y = jax.nn.sigmoid(op_ref[:, sl]) * (h_t.T * hn_ref[:, sl])
        y_ref[:, sl] = y.astype(BF16)
        kw = (kh.astype(F32) * w_c[:, h:h + 1]).astype(BF16)
        c_sc[h] = s_prev[h:h + 1, :] * c_prev + s_loc[h:h + 1, :] * _dot(vt, kw)
    m_sc[...] = m_new


def _mlstm_cell(q, k, v_t, o_pre, gc, gt, head_norm):
    s = q.shape[0]
    n = MLSTM_CHUNK
    ng = 2 * MLSTM_HEADS
    tri = jnp.tril(jnp.ones((n, n), BF16))
    row = lambda w: pl.BlockSpec((n, w), lambda c: (c, 0))
    return pl.pallas_call(
        _mlstm_cell_kernel,
        grid=(s // n,),
        in_specs=[row(MLSTM_QK_TOT), row(MLSTM_QK_TOT),
                  pl.BlockSpec((MLSTM_HEADS, MLSTM_V_EXT, n), lambda c: (0, 0, c)), row(D_MODEL), row(ng),
                  pl.BlockSpec((ng, n), lambda c: (0, c)), _const_spec((1, D_MODEL)), _const_spec((n, n))],
        out_specs=row(D_MODEL),
        out_shape=jax.ShapeDtypeStruct((s, D_MODEL), BF16),
        scratch_shapes=[pltpu.VMEM((MLSTM_HEADS, MLSTM_V_EXT, MLSTM_QK), F32), pltpu.VMEM((MLSTM_HEADS, 1), F32)],
        compiler_params=_params(1),
        name="mlstm_cell",
    )(q, k, v_t, o_pre, gc, gt, head_norm.reshape(1, -1), tri)


def _mlstm_out_router_kernel(x_ref, y_ref, wo_ref, g_ref, rt_ref, x3_ref, hn_ref, idx_ref, gate_ref):
    x3 = x_ref[...] + _dot(y_ref[...], wo_ref[...])
    hn = _rms(x3, g_ref[...])
    x3_ref[...] = x3
    hn_ref[...] = hn
    logits = lax.dot_general(rt_ref[...], hn, (((1,), (1,)), ((), ())), precision=lax.Precision.HIGHEST,
                             preferred_element_type=F32)
    e_iota = lax.broadcasted_iota(I32, logits.shape, 0)
    m1 = jnp.max(logits, axis=0, keepdims=True)
    i1 = jnp.min(jnp.where(logits == m1, e_iota, N_EXPERTS), axis=0, keepdims=True)
    rest = jnp.where(e_iota == i1, -jnp.inf, logits)
    m2 = jnp.max(rest, axis=0, keepdims=True)
    i2 = jnp.min(jnp.where(rest == m2, e_iota, N_EXPERTS), axis=0, keepdims=True)
    e2 = jnp.exp(m2 - m1)
    g1 = 1.0 / (1.0 + e2)
    idx_ref[...] = jnp.where(e_iota == 0, i1, jnp.where(e_iota == 1, i2, 0))
    gate_ref[...] = jnp.where(e_iota == 0, g1, jnp.where(e_iota == 1, e2 * g1, 0.0))


def _mlstm_out_router(x, y, w_out, g, router):
    s = x.shape[0]
    tm = TOKEN_TILE
    tok = pl.BlockSpec((tm, D_MODEL), lambda i: (i, 0))
    lane = pl.BlockSpec((N_EXPERTS, tm), lambda i: (0, i))
    return pl.pallas_call(
        _mlstm_out_router_kernel,
        grid=(s // tm,),
        in_specs=[tok, tok, _const_spec((D_MODEL, D_MODEL)), _const_spec((1, D_MODEL)),
                  _const_spec((N_EXPERTS, D_MODEL))],
        out_specs=[tok, tok, lane, lane],
        out_shape=[jax.ShapeDtypeStruct((s, D_MODEL), F32), jax.ShapeDtypeStruct((s, D_MODEL), F32),
                   jax.ShapeDtypeStruct((N_EXPERTS, s), I32), jax.ShapeDtypeStruct((N_EXPERTS, s), F32)],
        compiler_params=_params(1),
        name="mlstm_out_router",
    )(x, y, w_out.astype(BF16), g.reshape(1, -1), router.T)


def _row_gather_copy(src_hbm, dst, sem, src_row, dst_row):
    return pltpu.make_async_copy(src_hbm.at[pl.ds(src_row, 1), :], dst.at[pl.ds(dst_row, 1), :], sem)


def _moe_kernel(te_ref, nu_ref, tok_ref, tok_next_ref, hn_hbm, wg_ref, wu_ref, wd_ref, y_ref, xf_sc, xb_sc, sem):
    del te_ref
    tr = xb_sc.shape[0]
    t = pl.program_id(0)
    c = pl.program_id(1)
    nc = pl.num_programs(1)
    n_used = nu_ref[0]
    slot = lax.rem(t, 2)
    n_groups, rps = xf_sc.shape[1], xf_sc.shape[2]

    def wait_tile(s):
        for grp in range(n_groups):
            pltpu.make_async_copy(hn_hbm.at[pl.ds(0, rps), :], xf_sc.at[s, grp], sem.at[s]).wait()

    @pl.when((c == 0) & (t == 0) & (n_used > 0))
    def _():
        for grp in range(n_groups):
            def body(r, carry, grp=grp):
                _row_gather_copy(hn_hbm, xf_sc.at[0, grp], sem.at[0], tok_ref[0, 0, grp * rps + r], r).start()
                return carry
            lax.fori_loop(0, rps, body, 0, unroll=8)

    @pl.when((c == 0) & (t < n_used))
    def _():
        wait_tile(slot)
        for grp in range(n_groups):
            xb_sc[grp * rps:(grp + 1) * rps, :] = xf_sc[slot, grp].astype(BF16)

    @pl.when(c == 0)
    def _():
        y_ref[...] = jnp.zeros(y_ref.shape, F32)

    @pl.when(t < n_used)
    def _():
        xb = xb_sc[...]
        a = (_silu(_dot(xb, wg_ref[0])) * _dot(xb, wu_ref[0])).astype(BF16)
        dst = xf_sc.at[1 - slot, c]
        for r in range(rps):
            _row_gather_copy(hn_hbm, dst, sem.at[1 - slot], tok_next_ref[0, 0, c * rps + r], r).start()
        y_ref[...] += _dot(a, wd_ref[0])

    @pl.when((c == nc - 1) & (t == n_used - 1))
    def _():
        wait_tile(1 - slot)


def _moe_experts(hn, tok_sorted, tile_expert, n_used, wg, wu, wd):
    r = tok_sorted.shape[0]
    tr = MOE_ROW_TILE
    nt = r // tr
    nc = D_FF // FF_CHUNK
    tok3 = tok_sorted.reshape(nt, 1, tr)

    def chunk(t, c, nu):
        return jnp.where(t < nu[0], c, nc - 1)

    grid_spec = pltpu.PrefetchScalarGridSpec(
        num_scalar_prefetch=2,
        grid=(nt, nc),
        in_specs=[
            pl.BlockSpec((1, 1, tr), lambda t, c, te, nu: (t, 0, 0), memory_space=pltpu.SMEM),
            pl.BlockSpec((1, 1, tr), lambda t, c, te, nu: (jnp.minimum(t + 1, nt - 1), 0, 0),
                         memory_space=pltpu.SMEM),
            pl.BlockSpec(memory_space=pl.ANY),
            pl.BlockSpec((1, D_MODEL, FF_CHUNK), lambda t, c, te, nu: (te[t], 0, chunk(t, c, nu))),
            pl.BlockSpec((1, D_MODEL, FF_CHUNK), lambda t, c, te, nu: (te[t], 0, chunk(t, c, nu))),
            pl.BlockSpec((1, FF_CHUNK, D_MODEL), lambda t, c, te, nu: (te[t], chunk(t, c, nu), 0)),
        ],
        out_specs=pl.BlockSpec((tr, D_MODEL), lambda t, c, te, nu: (t, 0)),
        scratch_shapes=[pltpu.VMEM((2, nc, tr // nc, D_MODEL), F32), pltpu.VMEM((tr, D_MODEL), BF16),
                        pltpu.SemaphoreType.DMA((2,))],
    )
    return pl.pallas_call(
        _moe_kernel,
        grid_spec=grid_spec,
        out_shape=jax.ShapeDtypeStruct((r, D_MODEL), F32),
        compiler_params=_params(2),
        name="moe_experts",
    )(tile_expert, n_used, tok3, tok3, hn, wg.astype(BF16), wu.astype(BF16), wd.astype(BF16))


def _combine_kernel(r1_ref, r2_ref, x_ref, gate_ref, g_ref, y_hbm, out_ref, buf, sem):
    tm = x_ref.shape[0]
    i = pl.program_id(0)
    n_tiles = pl.num_programs(0) - 1

    @pl.when(i < n_tiles)
    def _():
        slot = lax.rem(i, 2)
        for r in range(tm):
            _row_gather_copy(y_hbm, buf.at[slot, 0], sem.at[slot], r1_ref[0, 0, r], r).start()
            _row_gather_copy(y_hbm, buf.at[slot, 1], sem.at[slot], r2_ref[0, 0, r], r).start()

    @pl.when(i > 0)
    def _():
        slot = lax.rem(i - 1, 2)
        pltpu.make_async_copy(y_hbm.at[pl.ds(0, tm), :], buf.at[slot, 0], sem.at[slot]).wait()
        pltpu.make_async_copy(y_hbm.at[pl.ds(0, tm), :], buf.at[slot, 1], sem.at[slot]).wait()
        gate = gate_ref[...]
        y = x_ref[...] + gate[:, 0:1] * buf[slot, 0] + gate[:, 1:2] * buf[slot, 1]
        out_ref[...] = _rms(y, g_ref[...])


def _moe_combine(x3, y_sorted, rows, gates, g):
    s = x3.shape[0]
    tm = TOKEN_TILE
    nt = s // tm
    r1 = rows[0].reshape(nt, 1, tm)
    r2 = rows[1].reshape(nt, 1, tm)
    started = pl.BlockSpec((1, 1, tm), lambda i: (jnp.minimum(i, nt - 1), 0, 0), memory_space=pltpu.SMEM)
    finished = lambda w: pl.BlockSpec((tm, w), lambda i: (jnp.maximum(i - 1, 0), 0))
    return pl.pallas_call(
        _combine_kernel,
        grid=(nt + 1,),
        in_specs=[started, started, finished(D_MODEL), finished(TOP_K), _const_spec((1, D_MODEL)),
                  pl.BlockSpec(memory_space=pl.ANY)],
        out_specs=finished(D_MODEL),
        out_shape=jax.ShapeDtypeStruct((s, D_MODEL), F32),
        scratch_shapes=[pltpu.VMEM((2, TOP_K, tm, D_MODEL), F32), pltpu.SemaphoreType.DMA((2,))],
        compiler_params=_params(1),
        name="moe_combine",
    )(r1, r2, x3, gates, g.reshape(1, -1), y_sorted)


def _route(idx, s):
    tr = MOE_ROW_TILE
    n_rows = TOP_K * s + N_EXPERTS * tr
    n_tiles = n_rows // tr
    e_flat = idx[:TOP_K].reshape(-1)
    onehot = (e_flat[:, None] == jnp.arange(N_EXPERTS, dtype=I32)[None, :]).astype(I32)
    rank = jnp.sum((jnp.cumsum(onehot, axis=0) - onehot) * onehot, axis=1)
    counts = jnp.sum(onehot, axis=0)
    tiles_per = (counts + tr - 1) // tr
    tile_end = jnp.cumsum(tiles_per)
    row_start = (tile_end - tiles_per) * tr
    dest = row_start[e_flat] + rank
    tok = jnp.tile(jnp.arange(s, dtype=I32), TOP_K)
    tok_sorted = jnp.zeros((n_rows,), I32).at[dest].set(tok, unique_indices=True)
    n_used = tile_end[-1]
    tile_ids = jnp.arange(n_tiles, dtype=I32)
    tile_expert = jnp.sum((tile_ids[:, None] >= tile_end[None, :]).astype(I32), axis=1)
    last_expert = jnp.sum((n_used - 1 >= tile_end).astype(I32))
    tile_expert = jnp.where(tile_ids < n_used, tile_expert, last_expert).astype(I32)
    return tok_sorted, tile_expert, n_used.reshape(1).astype(I32), dest.reshape(TOP_K, s).astype(I32)


def kernel(x, positions, norm_mix, norm_ffn, final_norm, mla_w_in, mla_q_norm, mla_w_qb, mla_kv_norm, mla_w_kvb,
           mla_w_out, mlstm_w_in, mlstm_conv_w, mlstm_conv_b, mlstm_gate_b, mlstm_head_norm, mlstm_w_out,
           ffn_w_gate, ffn_w_up, ffn_w_down, moe_router, moe_w_gate, moe_w_up, moe_w_down):
    b, s, d = x.shape
    assert b == 1 and d == D_MODEL and s % TOKEN_TILE == 0 and s % (ATTN_GROUP * ATTN_TILE) == 0
    assert norm_mix.shape[0] == 2, "one attention layer followed by one mLSTM layer"
    x0 = x[0]
    pos = positions[0]
    q_t, k, v_t = _mla_proj(x0, pos, norm_mix[0], mla_w_in[0], mla_q_norm[0], mla_w_qb[0], mla_kv_norm[0],
                            mla_w_kvb[0])
    o = _mla_attn(q_t, k, v_t)
    x2 = _attn_out_ffn(x0, o, mla_w_out[0], norm_ffn[0], ffn_w_gate[0], ffn_w_up[0], ffn_w_down[0])
    q, kk, v, o_pre, gc, gt = _mlstm_proj(x2, norm_mix[1], mlstm_w_in[0], mlstm_conv_w[0], mlstm_conv_b[0],
                                          mlstm_gate_b[0])
    y = _mlstm_cell(q, kk, v, o_pre, gc, gt, mlstm_head_norm[0])
    x3, hn, idx, gates = _mlstm_out_router(x2, y, mlstm_w_out[0], norm_ffn[1], moe_router[0])
    tok_sorted, tile_expert, n_used, rows = _route(idx, s)
    y_sorted = _moe_experts(hn, tok_sorted, tile_expert, n_used, moe_w_gate[0], moe_w_up[0], moe_w_down[0])
    out = _moe_combine(x3, y_sorted, rows, gates[:TOP_K].T, final_norm)
    return out[None]
```

```python
import functools

import jax
import jax.numpy as jnp
from jax import lax
from jax.experimental import pallas as pl
from jax.experimental.pallas import tpu as pltpu

F32 = jnp.float32
BF16 = jnp.bfloat16
I32 = jnp.int32

D_MODEL = 1024
EPS = 1e-6
MLA_HEADS = 8
MLA_Q_LORA = 256
MLA_KV_LORA = 128
MLA_NOPE = 128
MLA_ROPE = 64
MLA_V = 128
MLA_QK = MLA_NOPE + MLA_ROPE
MLA_QK_PAD = 256
MLA_V_EXT = MLA_V + 16
ROPE_THETA = 10000.0
MLSTM_HEADS = 8
MLSTM_V = 128
MLSTM_QK = 64
MLSTM_V_EXT = MLSTM_V + 16
MLSTM_QK_TOT = MLSTM_HEADS * MLSTM_QK
CONV_W = 4
GATE_CAP = 15.0
D_FF = 3584
N_EXPERTS = 8
TOP_K = 2

NEG = -1e30
LOG2_E = 1.4426950408889634

TOKEN_TILE = 512
ATTN_TILE = 512
ATTN_GROUP = 4
MLSTM_CHUNK = 256
MOE_ROW_TILE = 512
FF_CHUNK = 1792
HALO = 8
VMEM_LIMIT = 56 * 1024 * 1024


def _dot(a, b):
    return jnp.dot(a, b, preferred_element_type=F32)


def _dot_nt(a, b):
    return lax.dot_general(a, b, (((1,), (1,)), ((), ())), preferred_element_type=F32)


def _dot_tn(a, b):
    return lax.dot_general(a, b, (((0,), (0,)), ((), ())), preferred_element_type=F32)


def _rms(x, g):
    return x * lax.rsqrt(jnp.mean(x * x, axis=-1, keepdims=True) + EPS) * g


def _silu(x):
    return x * jax.nn.sigmoid(x)


def _split3(x):
    hi = x.astype(BF16)
    r1 = x - hi.astype(F32)
    mid = r1.astype(BF16)
    lo = (r1 - mid.astype(F32)).astype(BF16)
    return hi, mid, lo


def _params(n_axes):
    return pltpu.CompilerParams(dimension_semantics=("arbitrary",) * n_axes, vmem_limit_bytes=VMEM_LIMIT)


def _const_spec(shape):
    return pl.BlockSpec(shape, lambda *_: (0,) * len(shape))


def _mla_proj_kernel(x_ref, pos_ref, g_ref, inv_ref, win_ref, qn_ref, wqt_ref, kvn_ref, wk_ref, wvt_ref,
                     qt_ref, k_ref, vt_ref):
    tm = x_ref.shape[0]
    xn = _rms(x_ref[...], g_ref[...]).astype(BF16)
    proj = _dot(xn, win_ref[...])
    cq = _rms(proj[:, :MLA_Q_LORA], qn_ref[...]).astype(BF16)
    ckv = _rms(proj[:, MLA_Q_LORA:MLA_Q_LORA + MLA_KV_LORA], kvn_ref[...]).astype(BF16)
    ang = inv_ref[...] * pos_ref[...].astype(F32)
    cos_h = jnp.cos(ang)
    sin_h = jnp.sin(ang)
    cos_c = jnp.concatenate([cos_h, cos_h], axis=0)
    sin_c = jnp.concatenate([-sin_h, sin_h], axis=0)
    cos_r = cos_c.T
    sin_r = sin_c.T
    scale = MLA_QK ** -0.5 * LOG2_E
    n_nope = MLA_HEADS * MLA_NOPE
    n_rope = MLA_HEADS * MLA_ROPE
    q_t = _dot_nt(wqt_ref[...], cq)
    zq = jnp.zeros((MLA_QK_PAD - MLA_QK, tm), BF16)
    for h in range(MLA_HEADS):
        qt_ref[h, 0:MLA_NOPE, :] = (q_t[h * MLA_NOPE:(h + 1) * MLA_NOPE, :] * scale).astype(BF16)
        a = q_t[n_nope + h * MLA_ROPE:n_nope + (h + 1) * MLA_ROPE, :]
        b = q_t[n_nope + n_rope + h * MLA_ROPE:n_nope + n_rope + (h + 1) * MLA_ROPE, :]
        qt_ref[h, MLA_NOPE:MLA_QK, :] = ((a * cos_c + b * sin_c) * scale).astype(BF16)
        qt_ref[h, MLA_QK:MLA_QK_PAD, :] = zq
    k_nope = _dot(ckv, wk_ref[...])
    o_r = MLA_Q_LORA + MLA_KV_LORA
    k_rot = proj[:, o_r:o_r + MLA_ROPE] * cos_r + proj[:, o_r + MLA_ROPE:o_r + 2 * MLA_ROPE] * sin_r
    k_tail = jnp.concatenate([k_rot, jnp.zeros_like(k_rot)], axis=-1).astype(BF16)
    v_t = _dot_nt(wvt_ref[...], ckv)
    ones_rows = (lax.broadcasted_iota(I32, (MLA_V_EXT - MLA_V, tm), 0) == 0).astype(BF16)
    for h in range(MLA_HEADS):
        k_ref[h, :, 0:MLA_NOPE] = k_nope[:, h * MLA_NOPE:(h + 1) * MLA_NOPE].astype(BF16)
        k_ref[h, :, MLA_NOPE:MLA_QK_PAD] = k_tail
        vt_ref[h, 0, 0:MLA_V, :] = v_t[h * MLA_V:(h + 1) * MLA_V, :].astype(BF16)
        vt_ref[h, 0, MLA_V:MLA_V_EXT, :] = ones_rows


def _mla_proj(x, pos, g, w_in, q_norm, w_qb, kv_norm, w_kvb):
    s = x.shape[0]
    tm = ATTN_TILE
    nt = s // tm
    half = MLA_ROPE // 2
    o_r = MLA_Q_LORA + MLA_KV_LORA
    win = jnp.concatenate([w_in, w_in[:, o_r + half:], w_in[:, o_r:o_r + half]], axis=1).astype(BF16)
    wq = w_qb.reshape(MLA_Q_LORA, MLA_HEADS, MLA_QK)
    wq_rope = wq[:, :, MLA_NOPE:]
    wq_swap = jnp.concatenate([wq_rope[..., half:], wq_rope[..., :half]], axis=-1)
    wqt = jnp.concatenate([wq[:, :, :MLA_NOPE].reshape(MLA_Q_LORA, -1), wq_rope.reshape(MLA_Q_LORA, -1),
                           wq_swap.reshape(MLA_Q_LORA, -1)], axis=1).T.astype(BF16)
    wkv = w_kvb.reshape(MLA_KV_LORA, MLA_HEADS, MLA_NOPE + MLA_V)
    wk = wkv[:, :, :MLA_NOPE].reshape(MLA_KV_LORA, -1).astype(BF16)
    wvt = wkv[:, :, MLA_NOPE:].reshape(MLA_KV_LORA, -1).T.astype(BF16)
    inv = 1.0 / (ROPE_THETA ** (jnp.arange(0, MLA_ROPE, 2, dtype=F32) / MLA_ROPE))
    n_q = wqt.shape[0]
    return pl.pallas_call(
        _mla_proj_kernel,
        grid=(nt,),
        in_specs=[
            pl.BlockSpec((tm, D_MODEL), lambda i: (i, 0)),
            pl.BlockSpec((1, tm), lambda i: (0, i)),
            _const_spec((1, D_MODEL)),
            _const_spec((half, 1)),
            _const_spec((D_MODEL, 512)),
            _const_spec((1, MLA_Q_LORA)),
            _const_spec((n_q, MLA_Q_LORA)),
            _const_spec((1, MLA_KV_LORA)),
            _const_spec((MLA_KV_LORA, MLA_HEADS * MLA_NOPE)),
            _const_spec((MLA_HEADS * MLA_V, MLA_KV_LORA)),
        ],
        out_specs=[
            pl.BlockSpec((MLA_HEADS, MLA_QK_PAD, tm), lambda i: (0, 0, i)),
            pl.BlockSpec((MLA_HEADS, tm, MLA_QK_PAD), lambda i: (0, i, 0)),
            pl.BlockSpec((MLA_HEADS, 1, MLA_V_EXT, tm), lambda i: (0, i, 0, 0)),
        ],
        out_shape=[
            jax.ShapeDtypeStruct((MLA_HEADS, MLA_QK_PAD, s), BF16),
            jax.ShapeDtypeStruct((MLA_HEADS, s, MLA_QK_PAD), BF16),
            jax.ShapeDtypeStruct((MLA_HEADS, nt, MLA_V_EXT, tm), BF16),
        ],
        compiler_params=_params(1),
        name="mla_proj",
    )(x, pos.reshape(1, s), g.reshape(1, -1), inv.reshape(-1, 1), win, q_norm.reshape(1, -1), wqt,
      kv_norm.reshape(1, -1), wk, wvt)


def _attn_kernel(qt_ref, k_ref, vt_ref, o_ref, m_sc, acc_sc):
    t = ATTN_TILE
    g = pl.program_id(1)
    m_sc[...] = jnp.full(m_sc.shape, NEG, F32)
    acc_sc[...] = jnp.zeros(acc_sc.shape, F32)

    def scores(item):
        a, j, _ = item
        kj = k_ref[0, pl.ds(pl.multiple_of(j * t, t), t), :]
        return _dot(kj, qt_ref[0, :, a * t:(a + 1) * t])

    def update(item, s):
        a, j, diagonal = item
        if diagonal:
            kpos = lax.broadcasted_iota(I32, (t, t), 0)
            qpos = lax.broadcasted_iota(I32, (t, t), 1)
            s = jnp.where(kpos <= qpos, s, NEG)
        m_old = m_sc[a]
        m_new = jnp.maximum(m_old, jnp.max(s, axis=0, keepdims=True))
        p = jnp.exp2((s - m_new).astype(BF16))
        alpha = jnp.exp2(m_old - m_new)
        acc_sc[a] = alpha * acc_sc[a] + _dot(vt_ref[0, j], p)
        m_sc[a] = m_new

    def run(items):
        s_next = scores(items[0])
        for n, item in enumerate(items):
            s = s_next
            if n + 1 < len(items):
                s_next = scores(items[n + 1])
            update(item, s)

    def body(jj, carry):
        run([(a, ATTN_GROUP * jj + dj, False) for dj in range(ATTN_GROUP) for a in range(ATTN_GROUP)])
        return carry

    lax.fori_loop(0, g, body, 0)
    j0 = ATTN_GROUP * g
    run([(a, j0 + d, a == d) for d in range(ATTN_GROUP) for a in range(d, ATTN_GROUP)])
    for a in range(ATTN_GROUP):
        o_t = acc_sc[a, 0:MLA_V, :] * (1.0 / acc_sc[a, MLA_V:MLA_V + 1, :])
        o_ref[a * t:(a + 1) * t, :] = o_t.T.astype(BF16)


def _mla_attn(q_t, k, v_t):
    s = k.shape[1]
    t = ATTN_TILE
    nt = s // t
    tg = ATTN_GROUP * t
    return pl.pallas_call(
        _attn_kernel,
        grid=(MLA_HEADS, s // tg),
        in_specs=[
            pl.BlockSpec((1, MLA_QK_PAD, tg), lambda h, g: (h, 0, g)),
            pl.BlockSpec((1, s, MLA_QK_PAD), lambda h, g: (h, 0, 0)),
            pl.BlockSpec((1, nt, MLA_V_EXT, t), lambda h, g: (h, 0, 0, 0)),
        ],
        out_specs=pl.BlockSpec((tg, MLA_V), lambda h, g: (g, h)),
        out_shape=jax.ShapeDtypeStruct((s, MLA_HEADS * MLA_V), BF16),
        scratch_shapes=[pltpu.VMEM((ATTN_GROUP, 1, t), F32), pltpu.VMEM((ATTN_GROUP, MLA_V_EXT, t), F32)],
        compiler_params=_params(2),
        name="mla_attn",
    )(q_t, k, v_t)


def _attn_out_ffn_kernel(x_ref, o_ref, wo_ref, g_ref, wg_ref, wu_ref, wd_ref, out_ref):
    x1 = x_ref[...] + _dot(o_ref[...], wo_ref[...])
    h = _rms(x1, g_ref[...]).astype(BF16)
    acc = x1
    for c in range(D_FF // FF_CHUNK):
        sl = slice(c * FF_CHUNK, (c + 1) * FF_CHUNK)
        a = (_silu(_dot(h, wg_ref[:, sl])) * _dot(h, wu_ref[:, sl])).astype(BF16)
        acc = acc + _dot(a, wd_ref[sl, :])
    out_ref[...] = acc


def _resident_spec(shape):
    return pl.BlockSpec(shape, lambda *_: (0,) * len(shape), pipeline_mode=pl.Buffered(1))


def _attn_out_ffn(x, o, w_out, g, wg, wu, wd):
    s = x.shape[0]
    tm = TOKEN_TILE
    return pl.pallas_call(
        _attn_out_ffn_kernel,
        grid=(s // tm,),
        in_specs=[
            pl.BlockSpec((tm, D_MODEL), lambda i: (i, 0)),
            pl.BlockSpec((tm, D_MODEL), lambda i: (i, 0)),
            _resident_spec((D_MODEL, D_MODEL)),
            _const_spec((1, D_MODEL)),
            _resident_spec((D_MODEL, D_FF)),
            _resident_spec((D_MODEL, D_FF)),
            _resident_spec((D_FF, D_MODEL)),
        ],
        out_specs=pl.BlockSpec((tm, D_MODEL), lambda i: (i, 0)),
        out_shape=jax.ShapeDtypeStruct((s, D_MODEL), F32),
        compiler_params=_params(1),
        name="attn_out_ffn",
    )(x, o, w_out.astype(BF16), g.reshape(1, -1), wg.astype(BF16), wu.astype(BF16), wd.astype(BF16))


def _gate_act(g, is_input_gate):
    g = GATE_CAP * jnp.tanh(g / GATE_CAP)
    log_f = jnp.minimum(g, 0.0) - jnp.log1p(jnp.exp(-jnp.abs(g)))
    return jnp.where(is_input_gate, g, log_f)


def _mlstm_proj_kernel(x_ref, g_ref, wqk_ref, wv_ref, wo_ref, wgc_ref, wgt_ref, cw_ref, cb_ref, gbc_ref, gbr_ref,
                       q_ref, k_ref, v_ref, op_ref, gc_ref, gt_ref, ext_sc):
    tm = x_ref.shape[0]

    @pl.when(pl.program_id(0) == 0)
    def _():
        ext_sc[0:HALO, :] = jnp.zeros((HALO, ext_sc.shape[1]), F32)

    xn = _rms(x_ref[...], g_ref[...]).astype(BF16)
    ext_sc[HALO:HALO + tm, :] = _dot(xn, wqk_ref[...])
    conv = cb_ref[...] + cw_ref[CONV_W - 1:CONV_W, :] * ext_sc[HALO:HALO + tm, :]
    for back in range(1, CONV_W):
        w_row = cw_ref[CONV_W - 1 - back:CONV_W - back, :]
        conv = conv + w_row * ext_sc[HALO - back:HALO - back + tm, :]
    ext_sc[0:HALO, :] = ext_sc[tm:tm + HALO, :]
    qk = _silu(conv)
    q_ref[...] = (qk[:, :MLSTM_QK_TOT] * (MLSTM_QK ** -0.5)).astype(BF16)
    k_ref[...] = qk[:, MLSTM_QK_TOT:].astype(BF16)
    v_t = _dot_nt(wv_ref[...], xn)
    ones_rows = (lax.broadcasted_iota(I32, (MLSTM_V_EXT - MLSTM_V, tm), 0) == 0).astype(BF16)
    for h in range(MLSTM_HEADS):
        v_ref[h, 0:MLSTM_V, :] = v_t[h * MLSTM_V:(h + 1) * MLSTM_V, :].astype(BF16)
        v_ref[h, MLSTM_V:MLSTM_V_EXT, :] = ones_rows
    op_ref[...] = _dot(xn, wo_ref[...])
    gc = _dot(xn, wgc_ref[...]) + gbc_ref[...]
    gc_ref[...] = _gate_act(gc, lax.broadcasted_iota(I32, gc.shape, 1) < MLSTM_HEADS)
    gt = _dot_nt(wgt_ref[...], xn) + gbr_ref[...]
    gt_ref[...] = _gate_act(gt, lax.broadcasted_iota(I32, gt.shape, 0) < MLSTM_HEADS)


def _mlstm_proj(x, g, w_in, conv_w, conv_b, gate_b):
    s = x.shape[0]
    tm = TOKEN_TILE
    o1 = 2 * MLSTM_QK_TOT
    o2 = o1 + MLSTM_HEADS * MLSTM_V
    o3 = o2 + D_MODEL
    ng = 2 * MLSTM_HEADS
    w = w_in.astype(BF16)
    return pl.pallas_call(
        _mlstm_proj_kernel,
        grid=(s // tm,),
        in_specs=[
            pl.BlockSpec((tm, D_MODEL), lambda i: (i, 0)),
            _const_spec((1, D_MODEL)),
            _const_spec((D_MODEL, o1)), _const_spec((D_MODEL, o2 - o1)), _const_spec((D_MODEL, o3 - o2)),
            _const_spec((D_MODEL, ng)), _const_spec((ng, D_MODEL)),
            _const_spec((CONV_W, o1)), _const_spec((1, o1)),
            _const_spec((1, ng)), _const_spec((ng, 1)),
        ],
        out_specs=[
            pl.BlockSpec((tm, MLSTM_QK_TOT), lambda i: (i, 0)),
            pl.BlockSpec((tm, MLSTM_QK_TOT), lambda i: (i, 0)),
            pl.BlockSpec((MLSTM_HEADS, MLSTM_V_EXT, tm), lambda i: (0, 0, i)),
            pl.BlockSpec((tm, D_MODEL), lambda i: (i, 0)),
            pl.BlockSpec((tm, ng), lambda i: (i, 0)),
            pl.BlockSpec((ng, tm), lambda i: (0, i)),
        ],
        out_shape=[
            jax.ShapeDtypeStruct((s, MLSTM_QK_TOT), BF16),
            jax.ShapeDtypeStruct((s, MLSTM_QK_TOT), BF16),
            jax.ShapeDtypeStruct((MLSTM_HEADS, MLSTM_V_EXT, s), BF16),
            jax.ShapeDtypeStruct((s, D_MODEL), F32),
            jax.ShapeDtypeStruct((s, ng), F32),
            jax.ShapeDtypeStruct((ng, s), F32),
        ],
        scratch_shapes=[pltpu.VMEM((tm + HALO, o1), F32)],
        compiler_params=_params(1),
        name="mlstm_proj",
    )(x, g.reshape(1, -1), w[:, :o1], w[:, o1:o2].T, w[:, o2:o3], w[:, o3:], w[:, o3:].T, conv_w,
      conv_b.reshape(1, -1), gate_b.reshape(1, -1), gate_b.reshape(-1, 1))


def _mlstm_cell_kernel(q_ref, k_ref, vt_ref, op_ref, gc_ref, gt_ref, hn_ref, tri_ref, y_ref, c_sc, m_sc):
    n = q_ref.shape[0]
    nh = MLSTM_HEADS

    @pl.when(pl.program_id(0) == 0)
    def _():
        c_sc[...] = jnp.zeros(c_sc.shape, F32)
        m_sc[...] = jnp.zeros(m_sc.shape, F32)

    gc = gc_ref[...]
    gt = gt_ref[...]
    ig_c, lf_c = gc[:, :nh], gc[:, nh:]
    ig_r, lf_r = gt[:nh, :], gt[nh:, :]
    tri = tri_ref[...]
    b_c = sum(_dot(tri, part) for part in _split3(lf_c))
    b_r = sum(_dot_nt(part, tri) for part in _split3(lf_r))
    a_c = b_c[n - 1:n, :] - b_c + ig_c
    w_c = jnp.exp(a_c - jnp.max(a_c, axis=0, keepdims=True))
    rc_c = ig_c - b_c
    bl_r = b_r[:, n - 1:n]
    m_prev = m_sc[...]
    m_loc = jnp.max(bl_r - b_r + ig_r, axis=1, keepdims=True)
    m_new = jnp.maximum(bl_r + m_prev, m_loc)
    s_prev = jnp.exp(bl_r + m_prev - m_new)
    s_loc = jnp.exp(m_loc - m_new)
    inter_r = b_r + m_prev
    causal = lax.broadcasted_iota(I32, (n, n), 0) <= lax.broadcasted_iota(I32, (n, n), 1)
    for h in range(nh):
        qh = q_ref[:, h * MLSTM_QK:(h + 1) * MLSTM_QK]
        kh = k_ref[:, h * MLSTM_QK:(h + 1) * MLSTM_QK]
        vt = vt_ref[h]
        d = jnp.where(causal, rc_c[:, h:h + 1] + b_r[h:h + 1, :], NEG)
        inter = inter_r[h:h + 1, :]
        m_t = jnp.maximum(inter, jnp.max(d, axis=0, keepdims=True))
        wts = (jnp.exp(d - m_t) * _dot_nt(kh, qh)).astype(BF16)
        c_prev = c_sc[h]
        num = _dot(vt, wts) + jnp.exp(inter - m_t) * _dot_nt(c_prev.astype(BF16), qh)
        den = num[MLSTM_V:MLSTM_V + 1, :]
        h_t = num[0:MLSTM_V, :] / jnp.maximum(jnp.abs(den), jnp.exp(-m_t))
        h_t = h_t * lax.rsqrt(jnp.mean(h_t * h_t, axis=0, keepdims=True) + EPS)
        sl = slice(h * MLSTM_V, (h + 1) * MLSTM_V)
        y = jax.nn.sigmoid(op_ref[:, sl]) * (h_t.T * hn_ref[:, sl])
        y_ref[:, sl] = y.astype(BF16)
        kw = (kh.astype(F32) * w_c[:, h:h + 1]).astype(BF16)
        c_sc[h] = s_prev[h:h + 1, :] * c_prev + s_loc[h:h + 1, :] * _dot(vt, kw)
    m_sc[...] = m_new


def _mlstm_cell(q, k, v_t, o_pre, gc, gt, head_norm):
    s = q.shape[0]
    n = MLSTM_CHUNK
    ng = 2 * MLSTM_HEADS
    tri = jnp.tril(jnp.ones((n, n), BF16))
    row = lambda w: pl.BlockSpec((n, w), lambda c: (c, 0))
    return pl.pallas_call(
        _mlstm_cell_kernel,
        grid=(s // n,),
        in_specs=[row(MLSTM_QK_TOT), row(MLSTM_QK_TOT),
                  pl.BlockSpec((MLSTM_HEADS, MLSTM_V_EXT, n), lambda c: (0, 0, c)), row(D_MODEL), row(ng),
                  pl.BlockSpec((ng, n), lambda c: (0, c)), _const_spec((1, D_MODEL)), _const_spec((n, n))],
        out_specs=row(D_MODEL),
        out_shape=jax.ShapeDtypeStruct((s, D_MODEL), BF16),
        scratch_shapes=[pltpu.VMEM((MLSTM_HEADS, MLSTM_V_EXT, MLSTM_QK), F32), pltpu.VMEM((MLSTM_HEADS, 1), F32)],
        compiler_params=_params(1),
        name="mlstm_cell",
    )(q, k, v_t, o_pre, gc, gt, head_norm.reshape(1, -1), tri)


def _mlstm_out_router_kernel(x_ref, y_ref, wo_ref, g_ref, rt_ref, x3_ref, hn_ref, idx_ref, gate_ref):
    x3 = x_ref[...] + _dot(y_ref[...], wo_ref[...])
    hn = _rms(x3, g_ref[...])
    x3_ref[...] = x3
    hn_ref[...] = hn
    logits = lax.dot_general(rt_ref[...], hn, (((1,), (1,)), ((), ())), precision=lax.Precision.HIGHEST,
                             preferred_element_type=F32)
    e_iota = lax.broadcasted_iota(I32, logits.shape, 0)
    m1 = jnp.max(logits, axis=0, keepdims=True)
    i1 = jnp.min(jnp.where(logits == m1, e_iota, N_EXPERTS), axis=0, keepdims=True)
    rest = jnp.where(e_iota == i1, -jnp.inf, logits)
    m2 = jnp.max(rest, axis=0, keepdims=True)
    i2 = jnp.min(jnp.where(rest == m2, e_iota, N_EXPERTS), axis=0, keepdims=True)
    e2 = jnp.exp(m2 - m1)
    g1 = 1.0 / (1.0 + e2)
    idx_ref[...] = jnp.where(e_iota == 0, i1, jnp.where(e_iota == 1, i2, 0))
    gate_ref[...] = jnp.where(e_iota == 0, g1, jnp.where(e_iota == 1, e2 * g1, 0.0))


def _mlstm_out_router(x, y, w_out, g, router):
    s = x.shape[0]
    tm = TOKEN_TILE
    tok = pl.BlockSpec((tm, D_MODEL), lambda i: (i, 0))
    lane = pl.BlockSpec((N_EXPERTS, tm), lambda i: (0, i))
    return pl.pallas_call(
        _mlstm_out_router_kernel,
        grid=(s // tm,),
        in_specs=[tok, tok, _const_spec((D_MODEL, D_MODEL)), _const_spec((1, D_MODEL)),
                  _const_spec((N_EXPERTS, D_MODEL))],
        out_specs=[tok, tok, lane, lane],
        out_shape=[jax.ShapeDtypeStruct((s, D_MODEL), F32), jax.ShapeDtypeStruct((s, D_MODEL), F32),
                   jax.ShapeDtypeStruct((N_EXPERTS, s), I32), jax.ShapeDtypeStruct((N_EXPERTS, s), F32)],
        compiler_params=_params(1),
        name="mlstm_out_router",
    )(x, y, w_out.astype(BF16), g.reshape(1, -1), router.T)


def _row_gather_copy(src_hbm, dst, sem, src_row, dst_row):
    return pltpu.make_async_copy(src_hbm.at[pl.ds(src_row, 1), :], dst.at[pl.ds(dst_row, 1), :], sem)


def _moe_kernel(te_ref, nu_ref, tok_ref, tok_next_ref, hn_hbm, wg_ref, wu_ref, wd_ref, y_ref, xf_sc, xb_sc, sem):
    del te_ref
    tr = xb_sc.shape[0]
    t = pl.program_id(0)
    c = pl.program_id(1)
    nc = pl.num_programs(1)
    n_used = nu_ref[0]
    slot = lax.rem(t, 2)
    n_groups, rps = xf_sc.shape[1], xf_sc.shape[2]

    def wait_tile(s):
        for grp in range(n_groups):
            pltpu.make_async_copy(hn_hbm.at[pl.ds(0, rps), :], xf_sc.at[s, grp], sem.at[s]).wait()

    @pl.when((c == 0) & (t == 0) & (n_used > 0))
    def _():
        for grp in range(n_groups):
            def body(r, carry, grp=grp):
                _row_gather_copy(hn_hbm, xf_sc.at[0, grp], sem.at[0], tok_ref[0, 0, grp * rps + r], r).start()
                return carry
            lax.fori_loop(0, rps, body, 0, unroll=8)

    @pl.when((c == 0) & (t < n_used))
    def _():
        wait_tile(slot)
        for grp in range(n_groups):
            xb_sc[grp * rps:(grp + 1) * rps, :] = xf_sc[slot, grp].astype(BF16)

    @pl.when(c == 0)
    def _():
        y_ref[...] = jnp.zeros(y_ref.shape, F32)

    @pl.when(t < n_used)
    def _():
        xb = xb_sc[...]
        a = (_silu(_dot(xb, wg_ref[0])) * _dot(xb, wu_ref[0])).astype(BF16)
        dst = xf_sc.at[1 - slot, c]
        for r in range(rps):
            _row_gather_copy(hn_hbm, dst, sem.at[1 - slot], tok_next_ref[0, 0, c * rps + r], r).start()
        y_ref[...] += _dot(a, wd_ref[0])

    @pl.when((c == nc - 1) & (t == n_used - 1))
    def _():
        wait_tile(1 - slot)


def _moe_experts(hn, tok_sorted, tile_expert, n_used, wg, wu, wd):
    r = tok_sorted.shape[0]
    tr = MOE_ROW_TILE
    nt = r // tr
    nc = D_FF // FF_CHUNK
    tok3 = tok_sorted.reshape(nt, 1, tr)

    def chunk(t, c, nu):
        return jnp.where(t < nu[0], c, nc - 1)

    grid_spec = pltpu.PrefetchScalarGridSpec(
        num_scalar_prefetch=2,
        grid=(nt, nc),
        in_specs=[
            pl.BlockSpec((1, 1, tr), lambda t, c, te, nu: (t, 0, 0), memory_space=pltpu.SMEM),
            pl.BlockSpec((1, 1, tr), lambda t, c, te, nu: (jnp.minimum(t + 1, nt - 1), 0, 0),
                         memory_space=pltpu.SMEM),
            pl.BlockSpec(memory_space=pl.ANY),
            pl.BlockSpec((1, D_MODEL, FF_CHUNK), lambda t, c, te, nu: (te[t], 0, chunk(t, c, nu))),
            pl.BlockSpec((1, D_MODEL, FF_CHUNK), lambda t, c, te, nu: (te[t], 0, chunk(t, c, nu))),
            pl.BlockSpec((1, FF_CHUNK, D_MODEL), lambda t, c, te, nu: (te[t], chunk(t, c, nu), 0)),
        ],
        out_specs=pl.BlockSpec((tr, D_MODEL), lambda t, c, te, nu: (t, 0)),
        scratch_shapes=[pltpu.VMEM((2, nc, tr // nc, D_MODEL), F32), pltpu.VMEM((tr, D_MODEL), BF16),
                        pltpu.SemaphoreType.DMA((2,))],
    )
    return pl.pallas_call(
        _moe_kernel,
        grid_spec=grid_spec,
        out_shape=jax.ShapeDtypeStruct((r, D_MODEL), F32),
        compiler_params=_params(2),
        name="moe_experts",
    )(tile_expert, n_used, tok3, tok3, hn, wg.astype(BF16), wu.astype(BF16), wd.astype(BF16))


def _combine_kernel(r1_ref, r2_ref, x_ref, gate_ref, g_ref, y_hbm, out_ref, buf, sem):
    tm = x_ref.shape[0]
    i = pl.program_id(0)
    n_tiles = pl.num_programs(0) - 1

    @pl.when(i < n_tiles)
    def _():
        slot = lax.rem(i, 2)
        for r in range(tm):
            _row_gather_copy(y_hbm, buf.at[slot, 0], sem.at[slot], r1_ref[0, 0, r], r).start()
            _row_gather_copy(y_hbm, buf.at[slot, 1], sem.at[slot], r2_ref[0, 0, r], r).start()

    @pl.when(i > 0)
    def _():
        slot = lax.rem(i - 1, 2)
        pltpu.make_async_copy(y_hbm.at[pl.ds(0, tm), :], buf.at[slot, 0], sem.at[slot]).wait()
        pltpu.make_async_copy(y_hbm.at[pl.ds(0, tm), :], buf.at[slot, 1], sem.at[slot]).wait()
        gate = gate_ref[...]
        y = x_ref[...] + gate[:, 0:1] * buf[slot, 0] + gate[:, 1:2] * buf[slot, 1]
        out_ref[...] = _rms(y, g_ref[...])


def _moe_combine(x3, y_sorted, rows, gates, g):
    s = x3.shape[0]
    tm = TOKEN_TILE
    nt = s // tm
    r1 = rows[0].reshape(nt, 1, tm)
    r2 = rows[1].reshape(nt, 1, tm)
    started = pl.BlockSpec((1, 1, tm), lambda i: (jnp.minimum(i, nt - 1), 0, 0), memory_space=pltpu.SMEM)
    finished = lambda w: pl.BlockSpec((tm, w), lambda i: (jnp.maximum(i - 1, 0), 0))
    return pl.pallas_call(
        _combine_kernel,
        grid=(nt + 1,),
        in_specs=[started, started, finished(D_MODEL), finished(TOP_K), _const_spec((1, D_MODEL)),
                  pl.BlockSpec(memory_space=pl.ANY)],
        out_specs=finished(D_MODEL),
        out_shape=jax.ShapeDtypeStruct((s, D_MODEL), F32),
        scratch_shapes=[pltpu.VMEM((2, TOP_K, tm, D_MODEL), F32), pltpu.SemaphoreType.DMA((2,))],
        compiler_params=_params(1),
        name="moe_combine",
    )(r1, r2, x3, gates, g.reshape(1, -1), y_sorted)


def _route(idx, s):
    tr = MOE_ROW_TILE
    n_rows = TOP_K * s + N_EXPERTS * tr
    n_tiles = n_rows // tr
    e_flat = idx[:TOP_K].reshape(-1)
    onehot = (e_flat[:, None] == jnp.arange(N_EXPERTS, dtype=I32)[None, :]).astype(I32)
    rank = jnp.sum((jnp.cumsum(onehot, axis=0) - onehot) * onehot, axis=1)
    counts = jnp.sum(onehot, axis=0)
    tiles_per = (counts + tr - 1) // tr
    tile_end = jnp.cumsum(tiles_per)
    row_start = (tile_end - tiles_per) * tr
    dest = row_start[e_flat] + rank
    tok = jnp.tile(jnp.arange(s, dtype=I32), TOP_K)
    tok_sorted = jnp.zeros((n_rows,), I32).at[dest].set(tok, unique_indices=True)
    n_used = tile_end[-1]
    tile_ids = jnp.arange(n_tiles, dtype=I32)
    tile_expert = jnp.sum((tile_ids[:, None] >= tile_end[None, :]).astype(I32), axis=1)
    last_expert = jnp.sum((n_used - 1 >= tile_end).astype(I32))
    tile_expert = jnp.where(tile_ids < n_used, tile_expert, last_expert).astype(I32)
    return tok_sorted, tile_expert, n_used.reshape(1).astype(I32), dest.reshape(TOP_K, s).astype(I32)


def kernel(x, positions, norm_mix, norm_ffn, final_norm, mla_w_in, mla_q_norm, mla_w_qb, mla_kv_norm, mla_w_kvb,
           mla_w_out, mlstm_w_in, mlstm_conv_w, mlstm_conv_b, mlstm_gate_b, mlstm_head_norm, mlstm_w_out,
           ffn_w_gate, ffn_w_up, ffn_w_down, moe_router, moe_w_gate, moe_w_up, moe_w_down):
    b, s, d = x.shape
    assert b == 1 and d == D_MODEL and s % TOKEN_TILE == 0 and s % (ATTN_GROUP * ATTN_TILE) == 0
    assert norm_mix.shape[0] == 2, "one attention layer followed by one mLSTM layer"
    x0 = x[0]
    pos = positions[0]
    q_t, k, v_t = _mla_proj(x0, pos, norm_mix[0], mla_w_in[0], mla_q_norm[0], mla_w_qb[0], mla_kv_norm[0],
                            mla_w_kvb[0])
    o = _mla_attn(q_t, k, v_t)
    x2 = _attn_out_ffn(x0, o, mla_w_out[0], norm_ffn[0], ffn_w_gate[0], ffn_w_up[0], ffn_w_down[0])
    q, kk, v, o_pre, gc, gt = _mlstm_proj(x2, norm_mix[1], mlstm_w_in[0], mlstm_conv_w[0], mlstm_conv_b[0],
                                          mlstm_gate_b[0])
    y = _mlstm_cell(q, kk, v, o_pre, gc, gt, mlstm_head_norm[0])
    x3, hn, idx, gates = _mlstm_out_router(x2, y, mlstm_w_out[0], norm_ffn[1], moe_router[0])
    tok_sorted, tile_expert, n_used, rows = _route(idx, s)
    y_sorted = _moe_experts(hn, tok_sorted, tile_expert, n_used, moe_w_gate[0], moe_w_up[0], moe_w_down[0])
    out = _moe_combine(x3, y_sorted, rows, gates[:TOP_K].T, final_norm)
    return out[None]
```

```python
import functools

import jax
import jax.numpy as jnp
from jax import lax
from jax.experimental import pallas as pl
from jax.experimental.pallas import tpu as pltpu

F32 = jnp.float32
BF16 = jnp.bfloat16
I32 = jnp.int32

D_MODEL = 1024
EPS = 1e-6
MLA_HEADS = 8
MLA_Q_LORA = 256
MLA_KV_LORA = 128
MLA_NOPE = 128
MLA_ROPE = 64
MLA_V = 128
MLA_QK = MLA_NOPE + MLA_ROPE
MLA_QK_PAD = 256
MLA_V_EXT = MLA_V + 16
ROPE_THETA = 10000.0
MLSTM_HEADS = 8
MLSTM_V = 128
MLSTM_QK = 64
MLSTM_V_EXT = MLSTM_V + 16
MLSTM_QK_TOT = MLSTM_HEADS * MLSTM_QK
CONV_W = 4
GATE_CAP = 15.0
D_FF = 3584
N_EXPERTS = 8
TOP_K = 2

NEG = -1e30
LOG2_E = 1.4426950408889634

TOKEN_TILE = 512
ATTN_TILE = 512
ATTN_GROUP = 4
MLSTM_CHUNK = 256
MOE_ROW_TILE = 512
FF_CHUNK = 1792
HALO = 8
VMEM_LIMIT = 56 * 1024 * 1024


def _dot(a, b):
    return jnp.dot(a, b, preferred_element_type=F32)


def _dot_nt(a, b):
    return lax.dot_general(a, b, (((1,), (1,)), ((), ())), preferred_element_type=F32)


def _dot_tn(a, b):
    return lax.dot_general(a, b, (((0,), (0,)), ((), ())), preferred_element_type=F32)


def _rms(x, g):
    return x * lax.rsqrt(jnp.mean(x * x, axis=-1, keepdims=True) + EPS) * g


def _silu(x):
    return x * jax.nn.sigmoid(x)


def _split3(x):
    hi = x.astype(BF16)
    r1 = x - hi.astype(F32)
    mid = r1.astype(BF16)
    lo = (r1 - mid.astype(F32)).astype(BF16)
    return hi, mid, lo


def _params(n_axes):
    return pltpu.CompilerParams(dimension_semantics=("arbitrary",) * n_axes, vmem_limit_bytes=VMEM_LIMIT)


def _const_spec(shape):
    return pl.BlockSpec(shape, lambda *_: (0,) * len(shape))


def _mla_proj_kernel(x_ref, pos_ref, g_ref, inv_ref, win_ref, qn_ref, wqt_ref, kvn_ref, wk_ref, wvt_ref,
                     qt_ref, k_ref, vt_ref):
    tm = x_ref.shape[0]
    xn = _rms(x_ref[...], g_ref[...]).astype(BF16)
    proj = _dot(xn, win_ref[...])
    cq = _rms(proj[:, :MLA_Q_LORA], qn_ref[...]).astype(BF16)
    ckv = _rms(proj[:, MLA_Q_LORA:MLA_Q_LORA + MLA_KV_LORA], kvn_ref[...]).astype(BF16)
    ang = inv_ref[...] * pos_ref[...].astype(F32)
    cos_h = jnp.cos(ang)
    sin_h = jnp.sin(ang)
    cos_c = jnp.concatenate([cos_h, cos_h], axis=0)
    sin_c = jnp.concatenate([-sin_h, sin_h], axis=0)
    cos_r = cos_c.T
    sin_r = sin_c.T
    scale = MLA_QK ** -0.5 * LOG2_E
    n_nope = MLA_HEADS * MLA_NOPE
    n_rope = MLA_HEADS * MLA_ROPE
    q_t = _dot_nt(wqt_ref[...], cq)
    zq = jnp.zeros((MLA_QK_PAD - MLA_QK, tm), BF16)
    for h in range(MLA_HEADS):
        qt_ref[h, 0:MLA_NOPE, :] = (q_t[h * MLA_NOPE:(h + 1) * MLA_NOPE, :] * scale).astype(BF16)
        a = q_t[n_nope + h * MLA_ROPE:n_nope + (h + 1) * MLA_ROPE, :]
        b = q_t[n_nope + n_rope + h * MLA_ROPE:n_nope + n_rope + (h + 1) * MLA_ROPE, :]
        qt_ref[h, MLA_NOPE:MLA_QK, :] = ((a * cos_c + b * sin_c) * scale).astype(BF16)
        qt_ref[h, MLA_QK:MLA_QK_PAD, :] = zq
    k_nope = _dot(ckv, wk_ref[...])
    o_r = MLA_Q_LORA + MLA_KV_LORA
    k_rot = proj[:, o_r:o_r + MLA_ROPE] * cos_r + proj[:, o_r + MLA_ROPE:o_r + 2 * MLA_ROPE] * sin_r
    k_tail = jnp.concatenate([k_rot, jnp.zeros_like(k_rot)], axis=-1).astype(BF16)
    v_t = _dot_nt(wvt_ref[...], ckv)
    ones_rows = (lax.broadcasted_iota(I32, (MLA_V_EXT - MLA_V, tm), 0) == 0).astype(BF16)
    for h in range(MLA_HEADS):
        k_ref[h, :, 0:MLA_NOPE] = k_nope[:, h * MLA_NOPE:(h + 1) * MLA_NOPE].astype(BF16)
        k_ref[h, :, MLA_NOPE:MLA_QK_PAD] = k_tail
        vt_ref[h, 0, 0:MLA_V, :] = v_t[h * MLA_V:(h + 1) * MLA_V, :].astype(BF16)
        vt_ref[h, 0, MLA_V:MLA_V_EXT, :] = ones_rows


def _mla_proj(x, pos, g, w_in, q_norm, w_qb, kv_norm, w_kvb):
    s = x.shape[0]
    tm = ATTN_TILE
    nt = s // tm
    half = MLA_ROPE // 2
    o_r = MLA_Q_LORA + MLA_KV_LORA
    win = jnp.concatenate([w_in, w_in[:, o_r + half:], w_in[:, o_r:o_r + half]], axis=1).astype(BF16)
    wq = w_qb.reshape(MLA_Q_LORA, MLA_HEADS, MLA_QK)
    wq_rope = wq[:, :, MLA_NOPE:]
    wq_swap = jnp.concatenate([wq_rope[..., half:], wq_rope[..., :half]], axis=-1)
    wqt = jnp.concatenate([wq[:, :, :MLA_NOPE].reshape(MLA_Q_LORA, -1), wq_rope.reshape(MLA_Q_LORA, -1),
                           wq_swap.reshape(MLA_Q_LORA, -1)], axis=1).T.astype(BF16)
    wkv = w_kvb.reshape(MLA_KV_LORA, MLA_HEADS, MLA_NOPE + MLA_V)
    wk = wkv[:, :, :MLA_NOPE].reshape(MLA_KV_LORA, -1).astype(BF16)
    wvt = wkv[:, :, MLA_NOPE:].reshape(MLA_KV_LORA, -1).T.astype(BF16)
    inv = 1.0 / (ROPE_THETA ** (jnp.arange(0, MLA_ROPE, 2, dtype=F32) / MLA_ROPE))
    n_q = wqt.shape[0]
    return pl.pallas_call(
        _mla_proj_kernel,
        grid=(nt,),
        in_specs=[
            pl.BlockSpec((tm, D_MODEL), lambda i: (i, 0)),
            pl.BlockSpec((1, tm), lambda i: (0, i)),
            _const_spec((1, D_MODEL)),
            _const_spec((half, 1)),
            _const_spec((D_MODEL, 512)),
            _const_spec((1, MLA_Q_LORA)),
            _const_spec((n_q, MLA_Q_LORA)),
            _const_spec((1, MLA_KV_LORA)),
            _const_spec((MLA_KV_LORA, MLA_HEADS * MLA_NOPE)),
            _const_spec((MLA_HEADS * MLA_V, MLA_KV_LORA)),
        ],
        out_specs=[
            pl.BlockSpec((MLA_HEADS, MLA_QK_PAD, tm), lambda i: (0, 0, i)),
            pl.BlockSpec((MLA_HEADS, tm, MLA_QK_PAD), lambda i: (0, i, 0)),
            pl.BlockSpec((MLA_HEADS, 1, MLA_V_EXT, tm), lambda i: (0, i, 0, 0)),
        ],
        out_shape=[
            jax.ShapeDtypeStruct((MLA_HEADS, MLA_QK_PAD, s), BF16),
            jax.ShapeDtypeStruct((MLA_HEADS, s, MLA_QK_PAD), BF16),
            jax.ShapeDtypeStruct((MLA_HEADS, nt, MLA_V_EXT, tm), BF16),
        ],
        compiler_params=_params(1),
        name="mla_proj",
    )(x, pos.reshape(1, s), g.reshape(1, -1), inv.reshape(-1, 1), win, q_norm.reshape(1, -1), wqt,
      kv_norm.reshape(1, -1), wk, wvt)


def _attn_kernel(qt_ref, k_ref, vt_ref, o_ref, m_sc, acc_sc):
    t = ATTN_TILE
    g = pl.program_id(1)
    m_sc[...] = jnp.full(m_sc.shape, NEG, F32)
    acc_sc[...] = jnp.zeros(acc_sc.shape, F32)

    def scores(item):
        a, j, _ = item
        kj = k_ref[0, pl.ds(pl.multiple_of(j * t, t), t), :]
        return _dot(kj, qt_ref[0, :, a * t:(a + 1) * t])

    def update(item, s):
        a, j, diagonal = item
        if diagonal:
            kpos = lax.broadcasted_iota(I32, (t, t), 0)
            qpos = lax.broadcasted_iota(I32, (t, t), 1)
            s = jnp.where(kpos <= qpos, s, NEG)
        m_old = m_sc[a]
        m_new = jnp.maximum(m_old, jnp.max(s, axis=0, keepdims=True))
        p = jnp.exp2((s - m_new).astype(BF16))
        alpha = jnp.exp2(m_old - m_new)
        acc_sc[a] = alpha * acc_sc[a] + _dot(vt_ref[0, j], p)
        m_sc[a] = m_new

    def run(items):
        s_next = scores(items[0])
        for n, item in enumerate(items):
            s = s_next
            if n + 1 < len(items):
                s_next = scores(items[n + 1])
            update(item, s)

    def body(jj, carry):
        run([(a, ATTN_GROUP * jj + dj, False) for dj in range(ATTN_GROUP) for a in range(ATTN_GROUP)])
        return carry

    lax.fori_loop(0, g, body, 0)
    j0 = ATTN_GROUP * g
    run([(a, j0 + d, a == d) for d in range(ATTN_GROUP) for a in range(d, ATTN_GROUP)])
    for a in range(ATTN_GROUP):
        o_t = acc_sc[a, 0:MLA_V, :] * (1.0 / acc_sc[a, MLA_V:MLA_V + 1, :])
        o_ref[a * t:(a + 1) * t, :] = o_t.T.astype(BF16)


def _mla_attn(q_t, k, v_t):
    s = k.shape[1]
    t = ATTN_TILE
    nt = s // t
    tg = ATTN_GROUP * t
    return pl.pallas_call(
        _attn_kernel,
        grid=(MLA_HEADS, s // tg),
        in_specs=[
            pl.BlockSpec((1, MLA_QK_PAD, tg), lambda h, g: (h, 0, g)),
            pl.BlockSpec((1, s, MLA_QK_PAD), lambda h, g: (h, 0, 0)),
            pl.BlockSpec((1, nt, MLA_V_EXT, t), lambda h, g: (h, 0, 0, 0)),
        ],
        out_specs=pl.BlockSpec((tg, MLA_V), lambda h, g: (g, h)),
        out_shape=jax.ShapeDtypeStruct((s, MLA_HEADS * MLA_V), BF16),
        scratch_shapes=[pltpu.VMEM((ATTN_GROUP, 1, t), F32), pltpu.VMEM((ATTN_GROUP, MLA_V_EXT, t), F32)],
        compiler_params=_params(2),
        name="mla_attn",
    )(q_t, k, v_t)


def _attn_out_ffn_kernel(x_ref, o_ref, wo_ref, g_ref, wg_ref, wu_ref, wd_ref, out_ref):
    x1 = x_ref[...] + _dot(o_ref[...], wo_ref[...])
    h = _rms(x1, g_ref[...]).astype(BF16)
    acc = x1
    for c in range(D_FF // FF_CHUNK):
        sl = slice(c * FF_CHUNK, (c + 1) * FF_CHUNK)
        a = (_silu(_dot(h, wg_ref[:, sl])) * _dot(h, wu_ref[:, sl])).astype(BF16)
        acc = acc + _dot(a, wd_ref[sl, :])
    out_ref[...] = acc


def _resident_spec(shape):
    return pl.BlockSpec(shape, lambda *_: (0,) * len(shape), pipeline_mode=pl.Buffered(1))


def _attn_out_ffn(x, o, w_out, g, wg, wu, wd):
    s = x.shape[0]
    tm = TOKEN_TILE
    return pl.pallas_call(
        _attn_out_ffn_kernel,
        grid=(s // tm,),
        in_specs=[
            pl.BlockSpec((tm, D_MODEL), lambda i: (i, 0)),
            pl.BlockSpec((tm, D_MODEL), lambda i: (i, 0)),
            _resident_spec((D_MODEL, D_MODEL)),
            _const_spec((1, D_MODEL)),
            _resident_spec((D_MODEL, D_FF)),
            _resident_spec((D_MODEL, D_FF)),
            _resident_spec((D_FF, D_MODEL)),
        ],
        out_specs=pl.BlockSpec((tm, D_MODEL), lambda i: (i, 0)),
        out_shape=jax.ShapeDtypeStruct((s, D_MODEL), F32),
        compiler_params=_params(1),
        name="attn_out_ffn",
    )(x, o, w_out.astype(BF16), g.reshape(1, -1), wg.astype(BF16), wu.astype(BF16), wd.astype(BF16))


def _gate_act(g, is_input_gate):
    g = GATE_CAP * jnp.tanh(g / GATE_CAP)
    log_f = jnp.minimum(g, 0.0) - jnp.log1p(jnp.exp(-jnp.abs(g)))
    return jnp.where(is_input_gate, g, log_f)


def _mlstm_proj_kernel(x_ref, g_ref, wqk_ref, wv_ref, wo_ref, wgc_ref, wgt_ref, cw_ref, cb_ref, gbc_ref, gbr_ref,
                       q_ref, k_ref, v_ref, op_ref, gc_ref, gt_ref, ext_sc):
    tm = x_ref.shape[0]

    @pl.when(pl.program_id(0) == 0)
    def _():
        ext_sc[0:HALO, :] = jnp.zeros((HALO, ext_sc.shape[1]), F32)

    xn = _rms(x_ref[...], g_ref[...]).astype(BF16)
    ext_sc[HALO:HALO + tm, :] = _dot(xn, wqk_ref[...])
    conv = cb_ref[...] + cw_ref[CONV_W - 1:CONV_W, :] * ext_sc[HALO:HALO + tm, :]
    for back in range(1, CONV_W):
        w_row = cw_ref[CONV_W - 1 - back:CONV_W - back, :]
        conv = conv + w_row * ext_sc[HALO - back:HALO - back + tm, :]
    ext_sc[0:HALO, :] = ext_sc[tm:tm + HALO, :]
    qk = _silu(conv)
    q_ref[...] = (qk[:, :MLSTM_QK_TOT] * (MLSTM_QK ** -0.5)).astype(BF16)
    k_ref[...] = qk[:, MLSTM_QK_TOT:].astype(BF16)
    v_t = _dot_nt(wv_ref[...], xn)
    ones_rows = (lax.broadcasted_iota(I32, (MLSTM_V_EXT - MLSTM_V, tm), 0) == 0).astype(BF16)
    for h in range(MLSTM_HEADS):
        v_ref[h, 0:MLSTM_V, :] = v_t[h * MLSTM_V:(h + 1) * MLSTM_V, :].astype(BF16)
        v_ref[h, MLSTM_V:MLSTM_V_EXT, :] = ones_rows
    op_ref[...] = _dot(xn, wo_ref[...])
    gc = _dot(xn, wgc_ref[...]) + gbc_ref[...]
    gc_ref[...] = _gate_act(gc, lax.broadcasted_iota(I32, gc.shape, 1) < MLSTM_HEADS)
    gt = _dot_nt(wgt_ref[...], xn) + gbr_ref[...]
    gt_ref[...] = _gate_act(gt, lax.broadcasted_iota(I32, gt.shape, 0) < MLSTM_HEADS)


def _mlstm_proj(x, g, w_in, conv_w, conv_b, gate_b):
    s = x.shape[0]
    tm = TOKEN_TILE
    o1 = 2 * MLSTM_QK_TOT
    o2 = o1 + MLSTM_HEADS * MLSTM_V
    o3 = o2 + D_MODEL
    ng = 2 * MLSTM_HEADS
    w = w_in.astype(BF16)
    return pl.pallas_call(
        _mlstm_proj_kernel,
        grid=(s // tm,),
        in_specs=[
            pl.BlockSpec((tm, D_MODEL), lambda i: (i, 0)),
            _const_spec((1, D_MODEL)),
            _const_spec((D_MODEL, o1)), _const_spec((D_MODEL, o2 - o1)), _const_spec((D_MODEL, o3 - o2)),
            _const_spec((D_MODEL, ng)), _const_spec((ng, D_MODEL)),
            _const_spec((CONV_W, o1)), _const_spec((1, o1)),
            _const_spec((1, ng)), _const_spec((ng, 1)),
        ],
        out_specs=[
            pl.BlockSpec((tm, MLSTM_QK_TOT), lambda i: (i, 0)),
            pl.BlockSpec((tm, MLSTM_QK_TOT), lambda i: (i, 0)),
            pl.BlockSpec((MLSTM_HEADS, MLSTM_V_EXT, tm), lambda i: (0, 0, i)),
            pl.BlockSpec((tm, D_MODEL), lambda i: (i, 0)),
            pl.BlockSpec((tm, ng), lambda i: (i, 0)),
            pl.BlockSpec((ng, tm), lambda i: (0, i)),
        ],
        out_shape=[
            jax.ShapeDtypeStruct((s, MLSTM_QK_TOT), BF16),
            jax.ShapeDtypeStruct((s, MLSTM_QK_TOT), BF16),
            jax.ShapeDtypeStruct((MLSTM_HEADS, MLSTM_V_EXT, s), BF16),
            jax.ShapeDtypeStruct((s, D_MODEL), F32),
            jax.ShapeDtypeStruct((s, ng), F32),
            jax.ShapeDtypeStruct((ng, s), F32),
        ],
        scratch_shapes=[pltpu.VMEM((tm + HALO, o1), F32)],
        compiler_params=_params(1),
        name="mlstm_proj",
    )(x, g.reshape(1, -1), w[:, :o1], w[:, o1:o2].T, w[:, o2:o3], w[:, o3:], w[:, o3:].T, conv_w,
      conv_b.reshape(1, -1), gate_b.reshape(1, -1), gate_b.reshape(-1, 1))


def _mlstm_cell_kernel(q_ref, k_ref, vt_ref, op_ref, gc_ref, gt_ref, hn_ref, tri_ref, y_ref, c_sc, m_sc):
    n = q_ref.shape[0]
    nh = MLSTM_HEADS

    @pl.when(pl.program_id(0) == 0)
    def _():
        c_sc[...] = jnp.zeros(c_sc.shape, F32)
        m_sc[...] = jnp.zeros(m_sc.shape, F32)

    gc = gc_ref[...]
    gt = gt_ref[...]
    ig_c, lf_c = gc[:, :nh], gc[:, nh:]
    ig_r, lf_r = gt[:nh, :], gt[nh:, :]
    tri = tri_ref[...]
    b_c = sum(_dot(tri, part) for part in _split3(lf_c))
    b_r = sum(_dot_nt(part, tri) for part in _split3(lf_r))
    a_c = b_c[n - 1:n, :] - b_c + ig_c
    w_c = jnp.exp(a_c - jnp.max(a_c, axis=0, keepdims=True))
    rc_c = ig_c - b_c
    bl_r = b_r[:, n - 1:n]
    m_prev = m_sc[...]
    m_loc = jnp.max(bl_r - b_r + ig_r, axis=1, keepdims=True)
    m_new = jnp.maximum(bl_r + m_prev, m_loc)
    s_prev = jnp.exp(bl_r + m_prev - m_new)
    s_loc = jnp.exp(m_loc - m_new)
    inter_r = b_r + m_prev
    causal = lax.broadcasted_iota(I32, (n, n), 0) <= lax.broadcasted_iota(I32, (n, n), 1)
    for h in range(nh):
        qh = q_ref[:, h * MLSTM_QK:(h + 1) * MLSTM_QK]
        kh = k_ref[:, h * MLSTM_QK:(h + 1) * MLSTM_QK]
        vt = vt_ref[h]
        d = jnp.where(causal, rc_c[:, h:h + 1] + b_r[h:h + 1, :], NEG)
        inter = inter_r[h:h + 1, :]
        m_t = jnp.maximum(inter, jnp.max(d, axis=0, keepdims=True))
        wts = (jnp.exp(d - m_t) * _dot_nt(kh, qh)).astype(BF16)
        c_prev = c_sc[h]
        num = _dot(vt, wts) + jnp.exp(inter - m_t) * _dot_nt(c_prev.astype(BF16), qh)
        den = num[MLSTM_V:MLSTM_V + 1, :]
        h_t = num[0:MLSTM_V, :] / jnp.maximum(jnp.abs(den), jnp.exp(-m_t))
        h_t = h_t * lax.rsqrt(jnp.mean(h_t * h_t, axis=0, keepdims=True) + EPS)
        sl = slice(h * MLSTM_V, (h + 1) * MLSTM_V)
        y = jax.nn.sigmoid(op_ref[:, sl]) * (h_t.T * hn_ref[:, sl])
        y_ref[:, sl] = y.astype(BF16)
        kw = (kh.astype(F32) * w_c[:, h:h + 1]).astype(BF16)
        c_sc[h] = s_prev[h:h + 1, :] * c_prev + s_loc[h:h + 1, :] * _dot(vt, kw)
    m_sc[...] = m_new


def _mlstm_cell(q, k, v_t, o_pre, gc, gt, head_norm):
    s = q.shape[0]
    n = MLSTM_CHUNK
    ng = 2 * MLSTM_HEADS
    tri = jnp.tril(jnp.ones((n, n), BF16))
    row = lambda w: pl.BlockSpec((n, w), lambda c: (c, 0))
    return pl.pallas_call(
        _mlstm_cell_kernel,
        grid=(s // n,),
        in_specs=[row(MLSTM_QK_TOT), row(MLSTM_QK_TOT),
                  pl.BlockSpec((MLSTM_HEADS, MLSTM_V_EXT, n), lambda c: (0, 0, c)), row(D_MODEL), row(ng),
                  pl.BlockSpec((ng, n), lambda c: (0, c)), _const_spec((1, D_MODEL)), _const_spec((n, n))],
        out_specs=row(D_MODEL),
        out_shape=jax.ShapeDtypeStruct((s, D_MODEL), BF16),
        scratch_shapes=[pltpu.VMEM((MLSTM_HEADS, MLSTM_V_EXT, MLSTM_QK), F32), pltpu.VMEM((MLSTM_HEADS, 1), F32)],
        compiler_params=_params(1),
        name="mlstm_cell",
    )(q, k, v_t, o_pre, gc, gt, head_norm.reshape(1, -1), tri)


def _mlstm_out_router_kernel(x_ref, y_ref, wo_ref, g_ref, rt_ref, x3_ref, hn_ref, idx_ref, gate_ref):
    x3 = x_ref[...] + _dot(y_ref[...], wo_ref[...])
    hn = _rms(x3, g_ref[...])
    x3_ref[...] = x3
    hn_ref[...] = hn
    logits = lax.dot_general(rt_ref[...], hn, (((1,), (1,)), ((), ())), precision=lax.Precision.HIGHEST,
                             preferred_element_type=F32)
    e_iota = lax.broadcasted_iota(I32, logits.shape, 0)
    m1 = jnp.max(logits, axis=0, keepdims=True)
    i1 = jnp.min(jnp.where(logits == m1, e_iota, N_EXPERTS), axis=0, keepdims=True)
    rest = jnp.where(e_iota == i1, -jnp.inf, logits)
    m2 = jnp.max(rest, axis=0, keepdims=True)
    i2 = jnp.min(jnp.where(rest == m2, e_iota, N_EXPERTS), axis=0, keepdims=True)
    e2 = jnp.exp(m2 - m1)
    g1 = 1.0 / (1.0 + e2)
    idx_ref[...] = jnp.where(e_iota == 0, i1, jnp.where(e_iota == 1, i2, 0))
    gate_ref[...] = jnp.where(e_iota == 0, g1, jnp.where(e_iota == 1, e2 * g1, 0.0))


def _mlstm_out_router(x, y, w_out, g, router):
    s = x.shape[0]
    tm = TOKEN_TILE
    tok = pl.BlockSpec((tm, D_MODEL), lambda i: (i, 0))
    lane = pl.BlockSpec((N_EXPERTS, tm), lambda i: (0, i))
    return pl.pallas_call(
        _mlstm_out_router_kernel,
        grid=(s // tm,),
        in_specs=[tok, tok, _const_spec((D_MODEL, D_MODEL)), _const_spec((1, D_MODEL)),
                  _const_spec((N_EXPERTS, D_MODEL))],
        out_specs=[tok, tok, lane, lane],
        out_shape=[jax.ShapeDtypeStruct((s, D_MODEL), F32), jax.ShapeDtypeStruct((s, D_MODEL), F32),
                   jax.ShapeDtypeStruct((N_EXPERTS, s), I32), jax.ShapeDtypeStruct((N_EXPERTS, s), F32)],
        compiler_params=_params(1),
        name="mlstm_out_router",
    )(x, y, w_out.astype(BF16), g.reshape(1, -1), router.T)


def _row_gather_copy(src_hbm, dst, sem, src_row, dst_row):
    return pltpu.make_async_copy(src_hbm.at[pl.ds(src_row, 1), :], dst.at[pl.ds(dst_row, 1), :], sem)


def _moe_kernel(te_ref, nu_ref, tok_ref, tok_next_ref, hn_hbm, wg_ref, wu_ref, wd_ref, y_ref, xf_sc, xb_sc, sem):
    del te_ref
    tr = xb_sc.shape[0]
    t = pl.program_id(0)
    c = pl.program_id(1)
    nc = pl.num_programs(1)
    n_used = nu_ref[0]
    slot = lax.rem(t, 2)
    n_groups, rps = xf_sc.shape[1], xf_sc.shape[2]

    def wait_tile(s):
        for grp in range(n_groups):
            pltpu.make_async_copy(hn_hbm.at[pl.ds(0, rps), :], xf_sc.at[s, grp], sem.at[s]).wait()

    @pl.when((c == 0) & (t == 0) & (n_used > 0))
    def _():
        for grp in range(n_groups):
            def body(r, carry, grp=grp):
                _row_gather_copy(hn_hbm, xf_sc.at[0, grp], sem.at[0], tok_ref[0, 0, grp * rps + r], r).start()
                return carry
            lax.fori_loop(0, rps, body, 0, unroll=8)

    @pl.when((c == 0) & (t < n_used))
    def _():
        wait_tile(slot)
        for grp in range(n_groups):
            xb_sc[grp * rps:(grp + 1) * rps, :] = xf_sc[slot, grp].astype(BF16)

    @pl.when(c == 0)
    def _():
        y_ref[...] = jnp.zeros(y_ref.shape, F32)

    @pl.when(t < n_used)
    def _():
        xb = xb_sc[...]
        a = (_silu(_dot(xb, wg_ref[0])) * _dot(xb, wu_ref[0])).astype(BF16)
        dst = xf_sc.at[1 - slot, c]
        for r in range(rps):
            _row_gather_copy(hn_hbm, dst, sem.at[1 - slot], tok_next_ref[0, 0, c * rps + r], r).start()
        y_ref[...] += _dot(a, wd_ref[0])

    @pl.when((c == nc - 1) & (t == n_used - 1))
    def _():
        wait_tile(1 - slot)


def _moe_experts(hn, tok_sorted, tile_expert, n_used, wg, wu, wd):
    r = tok_sorted.shape[0]
    tr = MOE_ROW_TILE
    nt = r // tr
    nc = D_FF // FF_CHUNK
    tok3 = tok_sorted.reshape(nt, 1, tr)

    def chunk(t, c, nu):
        return jnp.where(t < nu[0], c, nc - 1)

    grid_spec = pltpu.PrefetchScalarGridSpec(
        num_scalar_prefetch=2,
        grid=(nt, nc),
        in_specs=[
            pl.BlockSpec((1, 1, tr), lambda t, c, te, nu: (t, 0, 0), memory_space=pltpu.SMEM),
            pl.BlockSpec((1, 1, tr), lambda t, c, te, nu: (jnp.minimum(t + 1, nt - 1), 0, 0),
                         memory_space=pltpu.SMEM),
            pl.BlockSpec(memory_space=pl.ANY),
            pl.BlockSpec((1, D_MODEL, FF_CHUNK), lambda t, c, te, nu: (te[t], 0, chunk(t, c, nu))),
            pl.BlockSpec((1, D_MODEL, FF_CHUNK), lambda t, c, te, nu: (te[t], 0, chunk(t, c, nu))),
            pl.BlockSpec((1, FF_CHUNK, D_MODEL), lambda t, c, te, nu: (te[t], chunk(t, c, nu), 0)),
        ],
        out_specs=pl.BlockSpec((tr, D_MODEL), lambda t, c, te, nu: (t, 0)),
        scratch_shapes=[pltpu.VMEM((2, nc, tr // nc, D_MODEL), F32), pltpu.VMEM((tr, D_MODEL), BF16),
                        pltpu.SemaphoreType.DMA((2,))],
    )
    return pl.pallas_call(
        _moe_kernel,
        grid_spec=grid_spec,
        out_shape=jax.ShapeDtypeStruct((r, D_MODEL), F32),
        compiler_params=_params(2),
        name="moe_experts",
    )(tile_expert, n_used, tok3, tok3, hn, wg.astype(BF16), wu.astype(BF16), wd.astype(BF16))


def _combine_kernel(r1_ref, r2_ref, x_ref, gate_ref, g_ref, y_hbm, out_ref, buf, sem):
    tm = x_ref.shape[0]
    i = pl.program_id(0)
    n_tiles = pl.num_programs(0) - 1

    @pl.when(i < n_tiles)
    def _():
        slot = lax.rem(i, 2)
        for r in range(tm):
            _row_gather_copy(y_hbm, buf.at[slot, 0], sem.at[slot], r1_ref[0, 0, r], r).start(priority=0)
            _row_gather_copy(y_hbm, buf.at[slot, 1], sem.at[slot], r2_ref[0, 0, r], r).start(priority=1)

    @pl.when(i > 0)
    def _():
        slot = lax.rem(i - 1, 2)
        pltpu.make_async_copy(y_hbm.at[pl.ds(0, tm), :], buf.at[slot, 0], sem.at[slot]).wait()
        pltpu.make_async_copy(y_hbm.at[pl.ds(0, tm), :], buf.at[slot, 1], sem.at[slot]).wait()
        gate = gate_ref[...]
        y = x_ref[...] + gate[:, 0:1] * buf[slot, 0] + gate[:, 1:2] * buf[slot, 1]
        out_ref[...] = _rms(y, g_ref[...])


def _moe_combine(x3, y_sorted, rows, gates, g):
    s = x3.shape[0]
    tm = TOKEN_TILE
    nt = s // tm
    r1 = rows[0].reshape(nt, 1, tm)
    r2 = rows[1].reshape(nt, 1, tm)
    started = pl.BlockSpec((1, 1, tm), lambda i: (jnp.minimum(i, nt - 1), 0, 0), memory_space=pltpu.SMEM)
    finished = lambda w: pl.BlockSpec((tm, w), lambda i: (jnp.maximum(i - 1, 0), 0))
    return pl.pallas_call(
        _combine_kernel,
        grid=(nt + 1,),
        in_specs=[started, started, finished(D_MODEL), finished(TOP_K), _const_spec((1, D_MODEL)),
                  pl.BlockSpec(memory_space=pl.ANY)],
        out_specs=finished(D_MODEL),
        out_shape=jax.ShapeDtypeStruct((s, D_MODEL), F32),
        scratch_shapes=[pltpu.VMEM((2, TOP_K, tm, D_MODEL), F32), pltpu.SemaphoreType.DMA((2,))],
        compiler_params=_params(1),
        name="moe_combine",
    )(r1, r2, x3, gates, g.reshape(1, -1), y_sorted)


def _route(idx, s):
    tr = MOE_ROW_TILE
    n_rows = TOP_K * s + N_EXPERTS * tr
    n_tiles = n_rows // tr
    e_flat = idx[:TOP_K].reshape(-1)
    onehot = (e_flat[:, None] == jnp.arange(N_EXPERTS, dtype=I32)[None, :]).astype(I32)
    rank = jnp.sum((jnp.cumsum(onehot, axis=0) - onehot) * onehot, axis=1)
    counts = jnp.sum(onehot, axis=0)
    tiles_per = (counts + tr - 1) // tr
    tile_end = jnp.cumsum(tiles_per)
    row_start = (tile_end - tiles_per) * tr
    dest = row_start[e_flat] + rank
    tok = jnp.tile(jnp.arange(s, dtype=I32), TOP_K)
    tok_sorted = jnp.zeros((n_rows,), I32).at[dest].set(tok, unique_indices=True)
    n_used = tile_end[-1]
    tile_ids = jnp.arange(n_tiles, dtype=I32)
    tile_expert = jnp.sum((tile_ids[:, None] >= tile_end[None, :]).astype(I32), axis=1)
    last_expert = jnp.sum((n_used - 1 >= tile_end).astype(I32))
    tile_expert = jnp.where(tile_ids < n_used, tile_expert, last_expert).astype(I32)
    return tok_sorted, tile_expert, n_used.reshape(1).astype(I32), dest.reshape(TOP_K, s).astype(I32)


def kernel(x, positions, norm_mix, norm_ffn, final_norm, mla_w_in, mla_q_norm, mla_w_qb, mla_kv_norm, mla_w_kvb,
           mla_w_out, mlstm_w_in, mlstm_conv_w, mlstm_conv_b, mlstm_gate_b, mlstm_head_norm, mlstm_w_out,
           ffn_w_gate, ffn_w_up, ffn_w_down, moe_router, moe_w_gate, moe_w_up, moe_w_down):
    b, s, d = x.shape
    assert b == 1 and d == D_MODEL and s % TOKEN_TILE == 0 and s % (ATTN_GROUP * ATTN_TILE) == 0
    assert norm_mix.shape[0] == 2, "one attention layer followed by one mLSTM layer"
    x0 = x[0]
    pos = positions[0]
    q_t, k, v_t = _mla_proj(x0, pos, norm_mix[0], mla_w_in[0], mla_q_norm[0], mla_w_qb[0], mla_kv_norm[0],
                            mla_w_kvb[0])
    o = _mla_attn(q_t, k, v_t)
    x2 = _attn_out_ffn(x0, o, mla_w_out[0], norm_ffn[0], ffn_w_gate[0], ffn_w_up[0], ffn_w_down[0])
    q, kk, v, o_pre, gc, gt = _mlstm_proj(x2, norm_mix[1], mlstm_w_in[0], mlstm_conv_w[0], mlstm_conv_b[0],
                                          mlstm_gate_b[0])
    y = _mlstm_cell(q, kk, v, o_pre, gc, gt, mlstm_head_norm[0])
    x3, hn, idx, gates = _mlstm_out_router(x2, y, mlstm_w_out[0], norm_ffn[1], moe_router[0])
    tok_sorted, tile_expert, n_used, rows = _route(idx, s)
    y_sorted = _moe_experts(hn, tok_sorted, tile_expert, n_used, moe_w_gate[0], moe_w_up[0], moe_w_down[0])
    out = _moe_combine(x3, y_sorted, rows, gates[:TOP_K].T, final_norm)
    return out[None]
```
